```python
import math
import jax
import jax.numpy as jnp
from jax import lax
import numpy as np

D_MODEL = 1024
BATCH = 4
SEQ = 4096
DEPTH = 2
DEC_BATCH = 128
DEC_SEQ = 8
PAST_LEN = 2048
PAGE_SIZE = 128

N_EVEN = (DEPTH + 1) // 2
N_ODD = DEPTH // 2
HEAD_DIM = 64
A_HEADS = 8
A_GROUPS = ((128, 1), (512, 4), (2048, 16))
A_MAX_WINDOW = 2048
A_QBLOCK = 128
A_WIDTH = A_HEADS * HEAD_DIM
B_HEADS = 8
B_DK = 64
B_DV = 64
B_CHUNK = 64
B_WIDTH = B_HEADS * B_DV
C_HEADS = 4
C_HD = 64
C_QBLOCK = 128
C_WIDTH = C_HEADS * 2 * C_HD
S5_GROUPS = 32
S5_GROUP_CH = 16
S5_STATE = 64
S5_WIDTH = S5_GROUPS * S5_GROUP_CH
EVEN_IN = 3 * A_WIDTH + 2 * B_HEADS * B_DK + 2 * B_WIDTH
EVEN_MIX = A_WIDTH + B_WIDTH
ODD_IN = 3 * C_WIDTH + S5_WIDTH
ODD_MIX = C_WIDTH + S5_WIDTH
D_FF = 2816
N_EXPERTS = 8
TOP_K = 2
EXPERT_FF = 2816
RMS_EPS = 1e-6

kernel_name = 'hybrid_dilated_hgrn2_diffattn_s5_step'


def rmsnorm(x, g):
    xf = x.astype(jnp.float32)
    y = xf * lax.rsqrt(jnp.mean(xf * xf, axis=-1, keepdims=True) + RMS_EPS)
    return (y * g.astype(jnp.float32)).astype(x.dtype)


def alibi_slopes(n):
    return jnp.asarray(2.0 ** (-8.0 * np.arange(1, n + 1) / n), dtype=jnp.float32)


def dilated_attention(q, k_ext, v_ext, q_start, n_pad):
    b, tq, h, hd = q.shape
    f32 = jnp.float32
    blk = A_QBLOCK if tq % A_QBLOCK == 0 else tq
    span = A_MAX_WINDOW + blk
    slopes = alibi_slopes(h)
    ti = jnp.arange(blk)

    def one_block(bi):
        t0 = bi * blk
        qb = lax.dynamic_slice_in_dim(q, t0, blk, axis=1).astype(f32) * hd ** -0.5
        start = q_start - A_MAX_WINDOW + t0
        kb = lax.dynamic_slice_in_dim(k_ext, start, span, axis=1)
        vb = lax.dynamic_slice_in_dim(v_ext, start, span, axis=1)
        outs, lses = [], []
        for window, dil in A_GROUPS:
            j = jnp.arange(window // dil + 1)
            loc = A_MAX_WINDOW + ti[:, None] - dil * j[None, :]
            valid = (start + loc) >= n_pad
            kg = jnp.take(kb, loc, axis=1).astype(f32)
            vg = jnp.take(vb, loc, axis=1).astype(f32)
            s = jnp.einsum('bthd,btjhd->bhtj', qb, kg) - slopes[:, None, None] * (dil * j).astype(f32)
            s = jnp.where(valid, s, -jnp.inf)
            m = jnp.max(s, axis=-1, keepdims=True)
            pr = jnp.exp(s - m)
            den = jnp.sum(pr, axis=-1)
            o = jnp.einsum('bhtj,btjhd->bthd', pr, vg) / jnp.swapaxes(den, 1, 2)[..., None]
            outs.append(o)
            lses.append(m[..., 0] + jnp.log(den))
        wts = jnp.swapaxes(jax.nn.softmax(jnp.stack(lses), axis=0), 2, 3)[..., None]
        return jnp.sum(wts * jnp.stack(outs), axis=0)

    ob = lax.map(one_block, jnp.arange(tq // blk))
    return jnp.moveaxis(ob, 0, 1).reshape(b, tq, h, hd)


def hgrn2_recurrence(q, k, v, log_f, s0):
    b, t, h, dk = q.shape
    dv = v.shape[-1]
    c = B_CHUNK if t % B_CHUNK == 0 else t
    n = t // c

    def chunks(a):
        return jnp.moveaxis(a.reshape(b, n, c, *a.shape[2:]), 1, 0)

    causal = jnp.tril(jnp.ones((c, c), dtype=bool))[None, :, :, None, None]

    def step(s, inp):
        qc, kc, vc, gc = inp
        cum = jnp.cumsum(gc, axis=1)
        o_inter = jnp.einsum('bthk,bhkv->bthv', qc * jnp.exp(cum), s)
        decay = jnp.exp(jnp.where(causal, cum[:, :, None] - cum[:, None, :], -jnp.inf))
        att = jnp.einsum('bthk,bshk,btshk->bhts', qc, kc, decay)
        o_intra = jnp.einsum('bhts,bshv->bthv', att, vc)
        last = cum[:, -1]
        s_new = jnp.exp(last)[..., None] * s + jnp.einsum('bshk,bshv->bhkv', kc * jnp.exp(last[:, None] - cum), vc)
        return s_new, o_inter + o_intra

    s_fin, o = lax.scan(step, s0, (chunks(q), chunks(k), chunks(v), chunks(log_f)))
    return jnp.moveaxis(o, 0, 1).reshape(b, t, h, dv), s_fin


def diff_attention(q, k_segs, v_segs, kpos_segs, q_pos, lam, slopes):
    f32 = jnp.float32
    qf = q.astype(f32) * C_HD ** -0.5
    q1, q2 = qf[..., :C_HD], qf[..., C_HD:]
    s1, s2 = [], []
    for k, kp in zip(k_segs, kpos_segs):
        kf = k.astype(f32)
        dist = (q_pos[:, None] - kp[None, :]).astype(f32)
        bias = jnp.where(dist >= 0, -slopes[:, None, None] * dist, -jnp.inf)
        s1.append(jnp.einsum('bqhd,bkhd->bhqk', q1, kf[..., :C_HD]) + bias)
        s2.append(jnp.einsum('bqhd,bkhd->bhqk', q2, kf[..., C_HD:]) + bias)
    a = jax.nn.softmax(jnp.concatenate(s1, -1), axis=-1) - lam * jax.nn.softmax(jnp.concatenate(s2, -1), axis=-1)
    out = 0.0
    off = 0
    for v in v_segs:
        nk = v.shape[1]
        out = out + jnp.einsum('bhqk,bkhd->bqhd', a[..., off:off + nk], v.astype(f32))
        off += nk
    return out


def diff_attention_blocked(q, k_segs, v_segs, kpos_segs, q_pos0, lam, slopes):
    b, t, h, dq = q.shape
    blk = C_QBLOCK if t % C_QBLOCK == 0 else t

    def one_block(bi):
        t0 = bi * blk
        qb = lax.dynamic_slice_in_dim(q, t0, blk, axis=1)
        return diff_attention(qb, k_segs, v_segs, kpos_segs, q_pos0 + t0 + jnp.arange(blk), lam, slopes)

    ob = lax.map(one_block, jnp.arange(t // blk))
    return jnp.moveaxis(ob, 0, 1).reshape(b, t, h, 2 * C_HD)


def complex_affine_combine(e1, e2):
    a1r, a1i, b1r, b1i = e1
    a2r, a2i, b2r, b2i = e2
    return (a2r * a1r - a2i * a1i, a2r * a1i + a2i * a1r,
            a2r * b1r - a2i * b1i + b2r, a2r * b1i + a2i * b1r + b2i)


def s5_scan(u, h0_re, h0_im, a_re, a_im, log_dt, b_re, b_im, c_re, c_im, d_skip):
    dt = jnp.exp(log_dt)[:, None]
    mag = jnp.exp(a_re * dt)
    ab_re, ab_im = mag * jnp.cos(a_im * dt), mag * jnp.sin(a_im * dt)
    den = a_re * a_re + a_im * a_im
    xr, xi = ab_re - 1.0, ab_im
    z_re = (xr * a_re + xi * a_im) / den
    z_im = (xi * a_re - xr * a_im) / den
    bb_re = z_re[..., None] * b_re - z_im[..., None] * b_im
    bb_im = z_re[..., None] * b_im + z_im[..., None] * b_re
    bu_re = jnp.einsum('btgc,gpc->btgp', u, bb_re)
    bu_im = jnp.einsum('btgc,gpc->btgp', u, bb_im)
    bu_re = bu_re.at[:, 0].add(ab_re * h0_re - ab_im * h0_im)
    bu_im = bu_im.at[:, 0].add(ab_re * h0_im + ab_im * h0_re)
    elems = (jnp.broadcast_to(ab_re, bu_re.shape), jnp.broadcast_to(ab_im, bu_re.shape), bu_re, bu_im)
    _, _, h_re, h_im = lax.associative_scan(complex_affine_combine, elems, axis=1)
    y = jnp.einsum('btgp,gcp->btgc', h_re, c_re) - jnp.einsum('btgp,gcp->btgc', h_im, c_im) + d_skip * u
    return y, h_re[:, -1], h_im[:, -1]


def even_mixer(h, l, p, past):
    e = l // 2
    f32 = jnp.float32
    b, t, _ = h.shape
    proj = h @ p['w_in_even'][e]
    cuts = list(np.cumsum([A_WIDTH, A_WIDTH, A_WIDTH, B_HEADS * B_DK, B_HEADS * B_DK, B_WIDTH]))
    qa, ka, va, qb, fb, ib, gb = jnp.split(proj, cuts, axis=-1)
    qa = qa.reshape(b, t, A_HEADS, HEAD_DIM)
    ka = ka.reshape(b, t, A_HEADS, HEAD_DIM)
    va = va.reshape(b, t, A_HEADS, HEAD_DIM)
    if past is None:
        pad = A_MAX_WINDOW
        k_ext = jnp.concatenate([jnp.zeros((b, pad, A_HEADS, HEAD_DIM), ka.dtype), ka], axis=1)
        v_ext = jnp.concatenate([jnp.zeros((b, pad, A_HEADS, HEAD_DIM), va.dtype), va], axis=1)
        q_start = pad
        keep = min(A_MAX_WINDOW, t)
        new_k, new_v = ka[:, t - keep:], va[:, t - keep:]
    else:
        kbuf, vbuf = p_get(past, 'a_k')[e], p_get(past, 'a_v')[e]
        a_buf = kbuf.shape[1]
        pad = A_MAX_WINDOW - a_buf
        k_ext = jnp.concatenate([jnp.zeros((b, pad, A_HEADS, HEAD_DIM), ka.dtype), kbuf.astype(ka.dtype), ka], axis=1)
        v_ext = jnp.concatenate([jnp.zeros((b, pad, A_HEADS, HEAD_DIM), va.dtype), vbuf.astype(va.dtype), va], axis=1)
        q_start = pad + a_buf
        new_k, new_v = ka, va
    o_a = dilated_attention(qa, k_ext, v_ext, q_start, pad)
    lb = jnp.cumsum(jax.nn.softmax(p['hgrn_lb'].astype(f32), axis=0), axis=0)[l].reshape(B_HEADS, B_DK)
    f = lb + (1.0 - lb) * jax.nn.sigmoid(fb.reshape(b, t, B_HEADS, B_DK).astype(f32))
    qh = jax.nn.silu(qb.reshape(b, t, B_HEADS, B_DK).astype(f32)) * B_DK ** -0.5
    vh = ib.reshape(b, t, B_HEADS, B_DV).astype(f32)
    if past is None:
        s0 = jnp.zeros((b, B_HEADS, B_DK, B_DV), f32)
    else:
        s0 = p_get(past, 'hgrn')[e].astype(f32)
    o_b, s_fin = hgrn2_recurrence(qh, 1.0 - f, vh, jnp.log(f), s0)
    o_b = rmsnorm(o_b, p['hgrn_gnorm'][e]) * jax.nn.silu(gb.reshape(b, t, B_HEADS, B_DV).astype(f32))
    mixed = jnp.concatenate([o_a.reshape(b, t, A_WIDTH), o_b.reshape(b, t, B_WIDTH)], axis=-1).astype(h.dtype)
    return mixed @ p['w_out_even'][e], {'a_k': new_k, 'a_v': new_v, 'hgrn': s_fin}


def p_get(d, name):
    return d[name]


def odd_mixer(h, l, p, past):
    o = l // 2
    f32 = jnp.float32
    b, t, _ = h.shape
    proj = h @ p['w_in_odd'][o]
    qc, kc, vc, u = jnp.split(proj, [C_WIDTH, 2 * C_WIDTH, 3 * C_WIDTH], axis=-1)
    qc = qc.reshape(b, t, C_HEADS, 2 * C_HD)
    kc = kc.reshape(b, t, C_HEADS, 2 * C_HD)
    vc = vc.reshape(b, t, C_HEADS, 2 * C_HD)
    lam_init = 0.8 - 0.6 * math.exp(-0.3 * l)
    lam = (jnp.exp(jnp.sum(p['diff_lq1'][o].astype(f32) * p['diff_lk1'][o].astype(f32)))
           - jnp.exp(jnp.sum(p['diff_lq2'][o].astype(f32) * p['diff_lk2'][o].astype(f32))) + lam_init)
    slopes = alibi_slopes(C_HEADS)
    if past is None:
        k_segs, v_segs, kpos = (kc,), (vc,), (jnp.arange(t),)
        q_pos0 = 0
    else:
        table = p_get(past, 'page_table')
        past_len = table.shape[1] * PAGE_SIZE
        k_past = p_get(past, 'c_k')[o][table].reshape(b, past_len, C_HEADS, 2 * C_HD)
        v_past = p_get(past, 'c_v')[o][table].reshape(b, past_len, C_HEADS, 2 * C_HD)
        k_segs, v_segs = (k_past, kc), (v_past, vc)
        kpos = (jnp.arange(past_len), past_len + jnp.arange(t))
        q_pos0 = past_len
    o_c = diff_attention_blocked(qc, k_segs, v_segs, kpos, q_pos0, lam, slopes)
    o_c = rmsnorm(o_c, p['diff_subln'][o]) * (1.0 - lam_init)
    uu = u.reshape(b, t, S5_GROUPS, S5_GROUP_CH).astype(f32)
    if past is None:
        h0_re = jnp.zeros((b, S5_GROUPS, S5_STATE), f32)
        h0_im = jnp.zeros((b, S5_GROUPS, S5_STATE), f32)
    else:
        h0_re = p_get(past, 's5_re')[o].astype(f32)
        h0_im = p_get(past, 's5_im')[o].astype(f32)
    y, h_re, h_im = s5_scan(uu, h0_re, h0_im,
                            p['s5_a_re'][o].astype(f32), p['s5_a_im'][o].astype(f32), p['s5_log_dt'][o].astype(f32),
                            p['s5_b_re'][o].astype(f32), p['s5_b_im'][o].astype(f32),
                            p['s5_c_re'][o].astype(f32), p['s5_c_im'][o].astype(f32), p['s5_d'][o].astype(f32))
    z = jax.nn.gelu(y.reshape(b, t, S5_WIDTH))
    o_d = z * jax.nn.sigmoid(z @ p['s5_w_glu'][o].astype(f32) + p['s5_b_glu'][o].astype(f32))
    mixed = jnp.concatenate([o_c.reshape(b, t, C_WIDTH), o_d], axis=-1).astype(h.dtype)
    return mixed @ p['w_out_odd'][o], {'c_k': kc, 'c_v': vc, 's5_re': h_re, 's5_im': h_im}


def swiglu(h, wg, wu, wd):
    return (jax.nn.silu(h @ wg) * (h @ wu)) @ wd


def moe_swiglu(h, router_w, router_b, wg, wu, wd):
    b, t, d = h.shape
    xf = h.reshape(b * t, d)
    logits = (xf @ router_w + router_b).astype(jnp.float32)
    top_v, top_i = lax.top_k(logits, TOP_K)
    gates = jax.nn.softmax(top_v, axis=-1)
    combine = jnp.sum(jax.nn.one_hot(top_i, N_EXPERTS, dtype=jnp.float32) * gates[..., None], axis=1)
    y = jnp.zeros_like(xf)
    for e in range(N_EXPERTS):
        y = y + combine[:, e:e + 1].astype(h.dtype) * swiglu(xf, wg[e], wu[e], wd[e])
    return y.reshape(b, t, d)


def trunk(x, p, past):
    new = {'a_k': [], 'a_v': [], 'hgrn': [], 'c_k': [], 'c_v': [], 's5_re': [], 's5_im': []}
    for l in range(DEPTH):
        hn = rmsnorm(x, p['norm_mix'][l])
        if l % 2 == 0:
            out, st = even_mixer(hn, l, p, past)
        else:
            out, st = odd_mixer(hn, l, p, past)
        for name in st:
            new[name].append(st[name])
        x = x + out
        hn = rmsnorm(x, p['norm_ffn'][l])
        if l % 2 == 0:
            e = l // 2
            x = x + swiglu(hn, p['ffn_w_gate'][e], p['ffn_w_up'][e], p['ffn_w_down'][e])
        else:
            o = l // 2
            x = x + moe_swiglu(hn, p['moe_router_w'][o], p['moe_router_b'][o],
                               p['moe_w_gate'][o], p['moe_w_up'][o], p['moe_w_down'][o])
    y = rmsnorm(x, p['norm_final'])
    return y, {name: jnp.stack(v) for name, v in new.items()}


def setup_inputs(seed: int = 0) -> dict:
    key = jax.random.key(seed)
    keys = iter(jax.random.split(key, 64))
    f32 = jnp.float32

    def nrm(shape, scale=1.0):
        return scale * jax.random.normal(next(keys), shape, f32)

    def gain(shape):
        return 1.0 + 0.01 * jax.random.normal(next(keys), shape, f32)

    n_pages = PAST_LEN // PAGE_SIZE
    n_used = DEC_BATCH * n_pages
    n_phys = n_used + max(1, n_used // 4)
    a_buf = min(A_MAX_WINDOW, PAST_LEN)
    page_table = jax.random.permutation(next(keys), n_phys)[:n_used].reshape(DEC_BATCH, n_pages).astype(jnp.int32)
    s5_im_init = jnp.pi * jnp.arange(S5_STATE, dtype=f32)[None, None, :]
    return {
        'x_prompt': nrm((BATCH, SEQ, D_MODEL)),
        'x_sample': nrm((DEC_BATCH, DEC_SEQ, D_MODEL)),
        'cache_a_k': nrm((N_EVEN, DEC_BATCH, a_buf, A_HEADS, HEAD_DIM)),
        'cache_a_v': nrm((N_EVEN, DEC_BATCH, a_buf, A_HEADS, HEAD_DIM)),
        'state_hgrn': nrm((N_EVEN, DEC_BATCH, B_HEADS, B_DK, B_DV), 0.5),
        'cache_c_k': nrm((N_ODD, n_phys, PAGE_SIZE, C_HEADS, 2 * C_HD)),
        'cache_c_v': nrm((N_ODD, n_phys, PAGE_SIZE, C_HEADS, 2 * C_HD)),
        'state_s5_re': nrm((N_ODD, DEC_BATCH, S5_GROUPS, S5_STATE), 0.5),
        'state_s5_im': nrm((N_ODD, DEC_BATCH, S5_GROUPS, S5_STATE), 0.5),
        'page_table': page_table,
        'norm_mix': gain((DEPTH, D_MODEL)),
        'norm_ffn': gain((DEPTH, D_MODEL)),
        'norm_final': gain((D_MODEL,)),
        'w_in_even': nrm((N_EVEN, D_MODEL, EVEN_IN), D_MODEL ** -0.5),
        'w_out_even': nrm((N_EVEN, EVEN_MIX, D_MODEL), EVEN_MIX ** -0.5),
        'hgrn_lb': nrm((DEPTH + 1, B_HEADS * B_DK), 0.1),
        'hgrn_gnorm': gain((N_EVEN, B_DV)),
        'ffn_w_gate': nrm((N_EVEN, D_MODEL, D_FF), D_MODEL ** -0.5),
        'ffn_w_up': nrm((N_EVEN, D_MODEL, D_FF), D_MODEL ** -0.5),
        'ffn_w_down': nrm((N_EVEN, D_FF, D_MODEL), D_FF ** -0.5),
        'w_in_odd': nrm((N_ODD, D_MODEL, ODD_IN), D_MODEL ** -0.5),
        'w_out_odd': nrm((N_ODD, ODD_MIX, D_MODEL), ODD_MIX ** -0.5),
        'diff_lq1': nrm((N_ODD, C_HD), 0.1),
        'diff_lk1': nrm((N_ODD, C_HD), 0.1),
        'diff_lq2': nrm((N_ODD, C_HD), 0.1),
        'diff_lk2': nrm((N_ODD, C_HD), 0.1),
        'diff_subln': gain((N_ODD, 2 * C_HD)),
        's5_a_re': -0.5 + nrm((N_ODD, S5_GROUPS, S5_STATE), 0.01),
        's5_a_im': s5_im_init + nrm((N_ODD, S5_GROUPS, S5_STATE), 0.01),
        's5_log_dt': jax.random.uniform(next(keys), (N_ODD, S5_GROUPS), f32, math.log(1e-3), math.log(1e-1)),
        's5_b_re': nrm((N_ODD, S5_GROUPS, S5_STATE, S5_GROUP_CH), (2 * S5_GROUP_CH) ** -0.5),
        's5_b_im': nrm((N_ODD, S5_GROUPS, S5_STATE, S5_GROUP_CH), (2 * S5_GROUP_CH) ** -0.5),
        's5_c_re': nrm((N_ODD, S5_GROUPS, S5_GROUP_CH, S5_STATE), (2 * S5_STATE) ** -0.5),
        's5_c_im': nrm((N_ODD, S5_GROUPS, S5_GROUP_CH, S5_STATE), (2 * S5_STATE) ** -0.5),
        's5_d': nrm((N_ODD, S5_GROUPS, S5_GROUP_CH)),
        's5_w_glu': nrm((N_ODD, S5_WIDTH, S5_WIDTH), S5_WIDTH ** -0.5),
        's5_b_glu': nrm((N_ODD, S5_WIDTH), 0.01),
        'moe_router_w': nrm((N_ODD, D_MODEL, N_EXPERTS), D_MODEL ** -0.5),
        'moe_router_b': nrm((N_ODD, N_EXPERTS), 0.01),
        'moe_w_gate': nrm((N_ODD, N_EXPERTS, D_MODEL, EXPERT_FF), D_MODEL ** -0.5),
        'moe_w_up': nrm((N_ODD, N_EXPERTS, D_MODEL, EXPERT_FF), D_MODEL ** -0.5),
        'moe_w_down': nrm((N_ODD, N_EXPERTS, EXPERT_FF, D_MODEL), EXPERT_FF ** -0.5),
    }


def reference(x_prompt, x_sample, cache_a_k, cache_a_v, state_hgrn, cache_c_k, cache_c_v, state_s5_re, state_s5_im,
              page_table, norm_mix, norm_ffn, norm_final, w_in_even, w_out_even, hgrn_lb, hgrn_gnorm,
              ffn_w_gate, ffn_w_up, ffn_w_down, w_in_odd, w_out_odd, diff_lq1, diff_lk1, diff_lq2, diff_lk2,
              diff_subln, s5_a_re, s5_a_im, s5_log_dt, s5_b_re, s5_b_im, s5_c_re, s5_c_im, s5_d, s5_w_glu,
              s5_b_glu, moe_router_w, moe_router_b, moe_w_gate, moe_w_up, moe_w_down):
    p = {
        'norm_mix': norm_mix, 'norm_ffn': norm_ffn, 'norm_final': norm_final,
        'w_in_even': w_in_even, 'w_out_even': w_out_even, 'hgrn_lb': hgrn_lb, 'hgrn_gnorm': hgrn_gnorm,
        'ffn_w_gate': ffn_w_gate, 'ffn_w_up': ffn_w_up, 'ffn_w_down': ffn_w_down,
        'w_in_odd': w_in_odd, 'w_out_odd': w_out_odd, 'diff_lq1': diff_lq1, 'diff_lk1': diff_lk1,
        'diff_lq2': diff_lq2, 'diff_lk2': diff_lk2, 'diff_subln': diff_subln,
        's5_a_re': s5_a_re, 's5_a_im': s5_a_im, 's5_log_dt': s5_log_dt, 's5_b_re': s5_b_re, 's5_b_im': s5_b_im,
        's5_c_re': s5_c_re, 's5_c_im': s5_c_im, 's5_d': s5_d, 's5_w_glu': s5_w_glu, 's5_b_glu': s5_b_glu,
        'moe_router_w': moe_router_w, 'moe_router_b': moe_router_b,
        'moe_w_gate': moe_w_gate, 'moe_w_up': moe_w_up, 'moe_w_down': moe_w_down,
    }
    past = {'a_k': cache_a_k, 'a_v': cache_a_v, 'hgrn': state_hgrn, 'c_k': cache_c_k, 'c_v': cache_c_v,
            's5_re': state_s5_re, 's5_im': state_s5_im, 'page_table': page_table}
    y_prompt, sp = trunk(x_prompt, p, None)
    y_sample, ss = trunk(x_sample, p, past)
    return (y_prompt, y_sample,
            sp['a_k'], sp['a_v'], sp['hgrn'], sp['c_k'], sp['c_v'], sp['s5_re'], sp['s5_im'],
            ss['a_k'], ss['a_v'], ss['hgrn'], ss['c_k'], ss['c_v'], ss['s5_re'], ss['s5_im'])
```

```python
import functools
import math

import jax
import jax.numpy as jnp
import numpy as np
from jax import lax
from jax.experimental import pallas as pl
from jax.experimental.pallas import tpu as pltpu

F32 = jnp.float32
BF16 = jnp.bfloat16

D_MODEL = 1024
HEAD_DIM = 64
A_HEADS = 8
A_GROUPS = ((128, 1), (512, 4), (2048, 16))
A_MAX_WINDOW = 2048
A_TILE = 128
B_HEADS = 8
B_DK = 64
C_HEADS = 4
C_HD = 64
S5_GROUPS = 32
S5_GROUP_CH = 16
S5_STATE = 64
N_EXPERTS = 8
PAGE_SIZE = 128
RMS_EPS = 1e-6
NEG = -1e30

LANES = 128
VMEM_LIMIT = 52 * 1024 * 1024


def _cparams(*sem):
    return pltpu.CompilerParams(dimension_semantics=sem, vmem_limit_bytes=VMEM_LIMIT)


def _dot(a, b):
    return jnp.dot(a, b, preferred_element_type=F32)


def _dot_nt(a, b):
    return lax.dot_general(a, b, (((1,), (1,)), ((), ())), preferred_element_type=F32)


def _dot_tn(a, b):
    return lax.dot_general(a, b, (((0,), (0,)), ((), ())), preferred_element_type=F32)


def _rms(x, g):
    return x * lax.rsqrt(jnp.mean(x * x, axis=-1, keepdims=True) + RMS_EPS) * g


def _alibi_slopes(n):
    return 2.0 ** (-8.0 * np.arange(1, n + 1) / n)


def _rms_matmul_kernel(x_ref, g_ref, w_ref, o_ref, xn_ref):
    @pl.when(pl.program_id(1) == 0)
    def _():
        xn_ref[...] = _rms(x_ref[...], g_ref[...]).astype(BF16)

    o_ref[...] = _dot(xn_ref[...], w_ref[...].astype(BF16))


def rms_matmul(x, g, w, *, tm, tn):
    n, d = x.shape
    nout = w.shape[1]
    return pl.pallas_call(
        _rms_matmul_kernel,
        grid=(n // tm, nout // tn),
        in_specs=[pl.BlockSpec((tm, d), lambda i, j: (i, 0)),
                  pl.BlockSpec((1, d), lambda i, j: (0, 0)),
                  pl.BlockSpec((d, tn), lambda i, j: (0, j))],
        out_specs=pl.BlockSpec((tm, tn), lambda i, j: (i, j)),
        out_shape=jax.ShapeDtypeStruct((n, nout), F32),
        scratch_shapes=[pltpu.VMEM((tm, d), BF16)],
        compiler_params=_cparams("parallel", "arbitrary"),
        name="rms_matmul",
    )(x, g.reshape(1, d), w)


def _mix_out_kernel(res_ref, a_ref, b_ref, wa_ref, wb_ref, o_ref):
    acc = _dot(a_ref[...].astype(BF16), wa_ref[...].astype(BF16))
    acc = acc + _dot(b_ref[...].astype(BF16), wb_ref[...].astype(BF16))
    o_ref[...] = res_ref[...] + acc


def mix_out(res, a, b, w, *, tm):
    n, d = res.shape
    ka = a.shape[1]
    return pl.pallas_call(
        _mix_out_kernel,
        grid=(n // tm,),
        in_specs=[pl.BlockSpec((tm, d), lambda i: (i, 0)),
                  pl.BlockSpec((tm, ka), lambda i: (i, 0)),
                  pl.BlockSpec((tm, ka), lambda i: (i, 0)),
                  pl.BlockSpec((ka, d), lambda i: (0, 0)),
                  pl.BlockSpec((ka, d), lambda i: (1, 0))],
        out_specs=pl.BlockSpec((tm, d), lambda i: (i, 0)),
        out_shape=jax.ShapeDtypeStruct((n, d), F32),
        compiler_params=_cparams("parallel"),
        name="mix_out",
    )(res, a, b, w, w)


def _ffn_kernel(x_ref, g_ref, wg_ref, wu_ref, wd_ref, o_ref, xn_ref, acc_ref):
    f = pl.program_id(1)

    @pl.when(f == 0)
    def _():
        xn_ref[...] = _rms(x_ref[...], g_ref[...]).astype(BF16)
        acc_ref[...] = jnp.zeros_like(acc_ref)

    xn = xn_ref[...]
    hg = _dot(xn, wg_ref[...].astype(BF16))
    hu = _dot(xn, wu_ref[...].astype(BF16))
    h = (jax.nn.silu(hg) * hu).astype(BF16)
    acc_ref[...] += _dot(h, wd_ref[...].astype(BF16))

    @pl.when(f == pl.num_programs(1) - 1)
    def _():
        o_ref[...] = x_ref[...] + acc_ref[...]


def ffn(x, g, wg, wu, wd, *, tm, tf):
    n, d = x.shape
    dff = wg.shape[1]
    return pl.pallas_call(
        _ffn_kernel,
        grid=(n // tm, dff // tf),
        in_specs=[pl.BlockSpec((tm, d), lambda i, f: (i, 0)),
                  pl.BlockSpec((1, d), lambda i, f: (0, 0)),
                  pl.BlockSpec((d, tf), lambda i, f: (0, f)),
                  pl.BlockSpec((d, tf), lambda i, f: (0, f)),
                  pl.BlockSpec((tf, d), lambda i, f: (f, 0))],
        out_specs=pl.BlockSpec((tm, d), lambda i, f: (i, 0)),
        out_shape=jax.ShapeDtypeStruct((n, d), F32),
        scratch_shapes=[pltpu.VMEM((tm, d), BF16), pltpu.VMEM((tm, d), F32)],
        compiler_params=_cparams("parallel", "arbitrary"),
        name="ffn",
    )(x, g.reshape(1, d), wg, wu, wd)


def _moe_kernel(x_ref, g_ref, rw_ref, rb_ref, wg_ref, wu_ref, wd_ref, o_ref, xn_ref, acc_ref, comb_ref):
    e = pl.program_id(1)
    f = pl.program_id(2)
    tm = x_ref.shape[0]
    lane = lax.broadcasted_iota(jnp.int32, (tm, LANES), 1)

    @pl.when((e == 0) & (f == 0))
    def _():
        hn = _rms(x_ref[...], g_ref[...])
        xn_ref[...] = hn.astype(BF16)
        acc_ref[...] = jnp.zeros_like(acc_ref)
        logits = jnp.dot(hn, rw_ref[...], precision=lax.Precision.HIGHEST,
                         preferred_element_type=F32) + rb_ref[...]
        lg = jnp.where(lane < N_EXPERTS, logits, NEG)
        m1 = jnp.max(lg, axis=1, keepdims=True)
        i1 = jnp.min(jnp.where(lg == m1, lane, LANES), axis=1, keepdims=True)
        lg2 = jnp.where(lane == i1, NEG, lg)
        m2 = jnp.max(lg2, axis=1, keepdims=True)
        i2 = jnp.min(jnp.where(lg2 == m2, lane, LANES), axis=1, keepdims=True)
        e2 = jnp.exp(m2 - m1)
        den = 1.0 + e2
        comb_ref[...] = jnp.where(lane == i1, 1.0 / den, 0.0) + jnp.where(lane == i2, e2 / den, 0.0)

    xn = xn_ref[...]
    hg = _dot(xn, wg_ref[...].astype(BF16))
    hu = _dot(xn, wu_ref[...].astype(BF16))
    h = (jax.nn.silu(hg) * hu).astype(BF16)
    y = _dot(h, wd_ref[...].astype(BF16))
    c = jnp.sum(jnp.where(lane == e, comb_ref[...], 0.0), axis=1, keepdims=True)
    acc_ref[...] += c * y

    @pl.when((e == pl.num_programs(1) - 1) & (f == pl.num_programs(2) - 1))
    def _():
        o_ref[...] = x_ref[...] + acc_ref[...]


def moe(x, g, rw, rb, wg, wu, wd, *, tm, tf):
    n, d = x.shape
    ne, _, dff = wg.shape
    rw_pad = jnp.zeros((d, LANES), F32).at[:, :ne].set(rw)
    rb_pad = jnp.zeros((1, LANES), F32).at[0, :ne].set(rb)
    return pl.pallas_call(
        _moe_kernel,
        grid=(n // tm, ne, dff // tf),
        in_specs=[pl.BlockSpec((tm, d), lambda i, e, f: (i, 0)),
                  pl.BlockSpec((1, d), lambda i, e, f: (0, 0)),
                  pl.BlockSpec((d, LANES), lambda i, e, f: (0, 0)),
                  pl.BlockSpec((1, LANES), lambda i, e, f: (0, 0)),
                  pl.BlockSpec((None, d, tf), lambda i, e, f: (e, 0, f)),
                  pl.BlockSpec((None, d, tf), lambda i, e, f: (e, 0, f)),
                  pl.BlockSpec((None, tf, d), lambda i, e, f: (e, f, 0))],
        out_specs=pl.BlockSpec((tm, d), lambda i, e, f: (i, 0)),
        out_shape=jax.ShapeDtypeStruct((n, d), F32),
        scratch_shapes=[pltpu.VMEM((tm, d), BF16), pltpu.VMEM((tm, d), F32), pltpu.VMEM((tm, LANES), F32)],
        compiler_params=_cparams("parallel", "arbitrary", "arbitrary"),
        name="moe",
    )(x, g.reshape(1, d), rw_pad, rb_pad, wg, wu, wd)


def _rmsnorm_kernel(x_ref, g_ref, o_ref):
    o_ref[...] = _rms(x_ref[...], g_ref[...])


def rmsnorm(x, g, *, tm):
    n, d = x.shape
    return pl.pallas_call(
        _rmsnorm_kernel,
        grid=(n // tm,),
        in_specs=[pl.BlockSpec((tm, d), lambda i: (i, 0)), pl.BlockSpec((1, d), lambda i: (0, 0))],
        out_specs=pl.BlockSpec((tm, d), lambda i: (i, 0)),
        out_shape=jax.ShapeDtypeStruct((n, d), F32),
        compiler_params=_cparams("parallel"),
        name="rmsnorm",
    )(x, g.reshape(1, d))


def _attn_a_prompt_kernel(q_ref, k_ref, v_ref, bias_ref, o_ref,
                          kx_ref, vx_ref, m0_ref, l0_ref, m1_ref, l1_ref, acc_ref, *, seq_len):
    pad = A_MAX_WINDOW
    kx_ref[0:pad, :] = jnp.zeros((pad, LANES), F32)
    vx_ref[0:pad, :] = jnp.zeros((pad, LANES), F32)
    kx_ref[pad:pad + seq_len, :] = k_ref[...]
    vx_ref[pad:pad + seq_len, :] = v_ref[...]
    lo = lax.broadcasted_iota(jnp.int32, (A_TILE, LANES), 1) < HEAD_DIM
    n_tiles = seq_len // A_TILE

    for g, (_, d) in enumerate(A_GROUPS):
        shift = int(math.log2(d))

        def tile(i, carry, g=g, d=d, shift=shift):
            m = lax.shift_right_logical(i, shift)
            r = i & (d - 1)
            row0 = r + (d * A_TILE) * m
            if d == 1:
                row0 = pl.multiple_of(row0, A_TILE)
                qsel = pl.ds(row0, A_TILE)
                ksel = pl.ds(row0 + pad - A_TILE, 2 * A_TILE)
            else:
                qsel = pl.ds(row0, A_TILE, stride=d)
                ksel = pl.ds(row0 + pad - A_TILE * d, 2 * A_TILE, stride=d)
            q = q_ref[qsel, :] * (HEAD_DIM ** -0.5)
            q2 = jnp.concatenate([jnp.where(lo, q, 0.0), jnp.where(lo, 0.0, q)], axis=0).astype(BF16)
            kb = kx_ref[ksel, :].astype(BF16)
            vb = vx_ref[ksel, :].astype(BF16)
            s = _dot_nt(q2, kb)
            first = jnp.where(m == 0, 0, 1)
            s0 = s[:A_TILE] + bias_ref[g, 0, first]
            s1 = s[A_TILE:] + bias_ref[g, 1, first]
            mt0 = jnp.max(s0, axis=1, keepdims=True)
            mt1 = jnp.max(s1, axis=1, keepdims=True)
            if g == 0:
                mn0 = jnp.broadcast_to(mt0, (A_TILE, LANES))
                mn1 = jnp.broadcast_to(mt1, (A_TILE, LANES))
            else:
                mo0 = m0_ref[qsel, :]
                mo1 = m1_ref[qsel, :]
                mn0 = jnp.maximum(mo0, mt0)
                mn1 = jnp.maximum(mo1, mt1)
            p0 = jnp.exp(s0 - jnp.concatenate([mn0, mn0], axis=1))
            p1 = jnp.exp(s1 - jnp.concatenate([mn1, mn1], axis=1))
            ls0 = jnp.sum(p0, axis=1, keepdims=True)
            ls1 = jnp.sum(p1, axis=1, keepdims=True)
            pv0 = _dot(p0.astype(BF16), vb)
            pv1 = _dot(p1.astype(BF16), vb)
            pv = jnp.where(lo, pv0, pv1)
            if g == 0:
                l0_ref[qsel, :] = jnp.broadcast_to(ls0, (A_TILE, LANES))
                l1_ref[qsel, :] = jnp.broadcast_to(ls1, (A_TILE, LANES))
                acc_ref[qsel, :] = pv
            else:
                a0 = jnp.exp(mo0 - mn0)
                a1 = jnp.exp(mo1 - mn1)
                l0_ref[qsel, :] = a0 * l0_ref[qsel, :] + ls0
                l1_ref[qsel, :] = a1 * l1_ref[qsel, :] + ls1
                acc_ref[qsel, :] = jnp.where(lo, a0, a1) * acc_ref[qsel, :] + pv
            m0_ref[qsel, :] = mn0
            m1_ref[qsel, :] = mn1
            return carry

        lax.fori_loop(0, n_tiles, tile, 0)

    lo_full = lax.broadcasted_iota(jnp.int32, (seq_len, LANES), 1) < HEAD_DIM
    o_ref[...] = acc_ref[...] / jnp.where(lo_full, l0_ref[...], l1_ref[...])


def _attn_a_prompt_bias():
    slopes = _alibi_slopes(A_HEADS)
    i = np.arange(A_TILE)[:, None]
    c = np.arange(2 * A_TILE)[None, :]
    delta = A_TILE + i - c
    band = (delta >= 0) & (delta <= A_TILE)
    out = np.zeros((A_HEADS // 2, len(A_GROUPS), 2, 2, A_TILE, 2 * A_TILE), np.float32)
    for hp in range(A_HEADS // 2):
        for g, (_, d) in enumerate(A_GROUPS):
            for hs in range(2):
                b = -slopes[2 * hp + hs] * d * delta
                out[hp, g, hs, 1] = np.where(band, b, NEG)
                out[hp, g, hs, 0] = np.where(band & (c >= A_TILE), b, NEG)
    return jnp.asarray(out)


def attn_a_prompt(proj, batch, seq_len):
    nhp = A_HEADS // 2
    kern = functools.partial(_attn_a_prompt_kernel, seq_len=seq_len)
    blk = lambda off: pl.BlockSpec((seq_len, LANES), lambda b, hp: (b, off + hp))
    stat = pltpu.VMEM((seq_len, LANES), F32)
    return pl.pallas_call(
        kern,
        grid=(batch, nhp),
        in_specs=[blk(0), blk(nhp), blk(2 * nhp),
                  pl.BlockSpec((None, len(A_GROUPS), 2, 2, A_TILE, 2 * A_TILE), lambda b, hp: (hp, 0, 0, 0, 0, 0))],
        out_specs=pl.BlockSpec((seq_len, LANES), lambda b, hp: (b, hp)),
        out_shape=jax.ShapeDtypeStruct((batch * seq_len, A_HEADS * HEAD_DIM), F32),
        scratch_shapes=[pltpu.VMEM((A_MAX_WINDOW + seq_len, LANES), F32),
                        pltpu.VMEM((A_MAX_WINDOW + seq_len, LANES), F32),
                        stat, stat, stat, stat, stat],
        compiler_params=_cparams("parallel", "parallel"),
        name="attn_a_prompt",
    )(proj, proj, proj, _attn_a_prompt_bias())


def _attn_a_sample_kernel(q_ref, kn_ref, vn_ref, kc_ref, vc_ref, bias_ref, mult_ref, o_ref, kx_ref, vx_ref,
                          *, t_new, n_cache):
    width = A_HEADS * HEAD_DIM
    n_ext = kx_ref.shape[0]
    tail = n_ext - n_cache - 16
    kx_ref[0:n_cache, :] = kc_ref[...].astype(BF16)
    vx_ref[0:n_cache, :] = vc_ref[...].astype(BF16)
    zpad = jnp.zeros((16 - t_new, width), F32)
    kx_ref[n_cache:n_cache + 16, :] = jnp.concatenate([kn_ref[...], zpad], axis=0).astype(BF16)
    vx_ref[n_cache:n_cache + 16, :] = jnp.concatenate([vn_ref[...], zpad], axis=0).astype(BF16)
    kx_ref[n_cache + 16:, :] = jnp.zeros((tail, width), BF16)
    vx_ref[n_cache + 16:, :] = jnp.zeros((tail, width), BF16)

    rows = A_HEADS * t_new
    rhead = lax.broadcasted_iota(jnp.int32, (rows, width), 0) // t_new
    lhead = lax.broadcasted_iota(jnp.int32, (rows, width), 1) // HEAD_DIM
    own = rhead == lhead
    q = q_ref[...] * (HEAD_DIM ** -0.5)
    qs = jnp.where(own, jnp.concatenate([q] * A_HEADS, axis=0), 0.0).astype(BF16)
    s = _dot_nt(qs, kx_ref[...]) + bias_ref[...]
    m = jnp.max(s, axis=1, keepdims=True)
    p = mult_ref[...] * jnp.exp(s - m)
    den = jnp.sum(p, axis=1, keepdims=True)
    o = _dot(p.astype(BF16), vx_ref[...]) / den
    o = jnp.where(own, o, 0.0).reshape(A_HEADS, t_new, width)
    o_ref[...] = jnp.sum(o, axis=0)


def _attn_a_sample_consts(t_new, n_cache, n_ext):
    slopes = _alibi_slopes(A_HEADS)
    i = np.arange(t_new)[:, None]
    row = np.arange(n_ext)[None, :]
    delta = n_cache + i - row
    mult = np.zeros((t_new, n_ext), np.float32)
    for window, d in A_GROUPS:
        mult += ((delta >= 0) & (delta % d == 0) & (delta <= window) & (row < n_cache + t_new)).astype(np.float32)
    bias = np.stack([np.where(mult > 0, -slopes[h] * delta, NEG) for h in range(A_HEADS)])
    mult = np.broadcast_to(mult[None], (A_HEADS, t_new, n_ext))
    return (jnp.asarray(bias.reshape(A_HEADS * t_new, n_ext), F32),
            jnp.asarray(mult.reshape(A_HEADS * t_new, n_ext), F32))


def attn_a_sample(proj, cache_k, cache_v, n_seq, t_new):
    width = A_HEADS * HEAD_DIM
    n_cache = cache_k.shape[1]
    assert n_cache == A_MAX_WINDOW, "the dilated-attention sample kernel expects a full window buffer"
    n_ext = n_cache + LANES
    bias, mult = _attn_a_sample_consts(t_new, n_cache, n_ext)
    kern = functools.partial(_attn_a_sample_kernel, t_new=t_new, n_cache=n_cache)
    new = lambda col: pl.BlockSpec((t_new, width), lambda s: (s, col))
    cache = pl.BlockSpec((None, n_cache, width), lambda s: (s, 0, 0))
    const = pl.BlockSpec((A_HEADS * t_new, n_ext), lambda s: (0, 0))
    return pl.pallas_call(
        kern,
        grid=(n_seq,),
        in_specs=[new(0), new(1), new(2), cache, cache, const, const],
        out_specs=pl.BlockSpec((t_new, width), lambda s: (s, 0)),
        out_shape=jax.ShapeDtypeStruct((n_seq * t_new, width), F32),
        scratch_shapes=[pltpu.VMEM((n_ext, width), BF16), pltpu.VMEM((n_ext, width), BF16)],
        compiler_params=_cparams("parallel"),
        name="attn_a_sample",
    )(proj, proj, proj, cache_k, cache_v, bias, mult)


def _hgrn_chunk(qraw, fraw, iv, graw, st, lb, gn, *, chunk, sub, valid):
    lo = lax.broadcasted_iota(jnp.int32, (chunk, LANES), 1) < B_DK
    rowi = lax.broadcasted_iota(jnp.int32, (chunk, LANES), 0)
    f = lb + (1.0 - lb) * jax.nn.sigmoid(fraw)
    logf = jnp.log(f)
    kk = 1.0 - f
    if valid < chunk:
        logf = jnp.where(rowi < valid, logf, 0.0)
        kk = jnp.where(rowi < valid, kk, 0.0)
    qh = jax.nn.silu(qraw) * (B_DK ** -0.5)
    tri_r = lax.broadcasted_iota(jnp.int32, (chunk, chunk), 0)
    tri_c = lax.broadcasted_iota(jnp.int32, (chunk, chunk), 1)
    causal = tri_c <= tri_r
    cum = jnp.dot(causal.astype(F32), logf, precision=lax.Precision.HIGHEST, preferred_element_type=F32)
    last = cum[chunk - 1:chunk, :]

    o = _dot_nt((qh * jnp.exp(cum)).astype(BF16), st.astype(BF16))

    att0, att1 = [], []
    lo_sub = lax.broadcasted_iota(jnp.int32, (sub, LANES), 1) < B_DK
    for blk in range(chunk // sub):
        r0 = blk * sub
        n = r0 + sub
        c0 = cum[r0 - 1:r0, :] if blk > 0 else jnp.zeros((1, LANES), F32)
        qt = qh[r0:n] * jnp.exp(cum[r0:n] - c0)
        q2 = jnp.concatenate([jnp.where(lo_sub, qt, 0.0), jnp.where(lo_sub, 0.0, qt)], axis=0).astype(BF16)
        kt = jnp.where(rowi < n, kk * jnp.exp(jnp.where(rowi < n, c0 - cum, 0.0)), 0.0).astype(BF16)
        a = _dot_nt(q2, kt)
        att0.append(a[:sub])
        att1.append(a[sub:])
    att0 = jnp.where(causal, jnp.concatenate(att0, axis=0), 0.0).astype(BF16)
    att1 = jnp.where(causal, jnp.concatenate(att1, axis=0), 0.0).astype(BF16)
    o = o + _dot(att0, jnp.where(lo, iv, 0.0).astype(BF16)) + _dot(att1, jnp.where(lo, 0.0, iv).astype(BF16))

    kend = (kk * jnp.exp(last - cum)).astype(BF16)
    upd = _dot_tn(iv.astype(BF16), kend)
    same = ((lax.broadcasted_iota(jnp.int32, (LANES, LANES), 0) < B_DK)
            == (lax.broadcasted_iota(jnp.int32, (LANES, LANES), 1) < B_DK))
    st_new = st * jnp.exp(last) + jnp.where(same, upd, 0.0)

    o2 = o * o
    ms0 = jnp.sum(jnp.where(lo, o2, 0.0), axis=1, keepdims=True)
    ms1 = jnp.sum(jnp.where(lo, 0.0, o2), axis=1, keepdims=True)
    ms = jnp.where(lo, ms0, ms1) * (1.0 / B_DK)
    out = o * lax.rsqrt(ms + RMS_EPS) * gn * jax.nn.silu(graw)
    return out, st_new


def _hgrn_kernel(q_ref, f_ref, i_ref, g_ref, lb_ref, gn_ref, s0_ref, o_ref, sout_ref, *, n_seq, t_len, chunk, sub):
    lb = lb_ref[...]
    gn = gn_ref[...]
    valid = min(t_len, chunk)
    n_chunks = max(t_len // chunk, 1)
    zero = jnp.zeros((B_DK, B_DK), F32)

    def one_seq(sq, carry):
        s_bd = jnp.concatenate([jnp.concatenate([s0_ref[sq, 0], zero], axis=1),
                                jnp.concatenate([zero, s0_ref[sq, 1]], axis=1)], axis=0)
        st = s_bd.T

        def one_chunk(c, st):
            base = pl.multiple_of(sq * t_len + c * valid, 8)
            rows = pl.ds(base, valid)
            args = [r[rows, :] for r in (q_ref, f_ref, i_ref, g_ref)]
            if valid < chunk:
                args = [jnp.concatenate([a, jnp.zeros((chunk - valid, LANES), F32)], axis=0) for a in args]
            out, st = _hgrn_chunk(*args, st, lb, gn, chunk=chunk, sub=sub, valid=valid)
            o_ref[rows, :] = out[:valid]
            return st

        st = lax.fori_loop(0, n_chunks, one_chunk, st)
        s_fin = st.T
        sout_ref[sq, 0] = s_fin[:B_DK, :B_DK]
        sout_ref[sq, 1] = s_fin[B_DK:, B_DK:]
        return carry

    lax.fori_loop(0, n_seq, one_seq, 0)


def hgrn(proj, lb, gnorm, s0, *, n_seq, t_len, seq_per_step, chunk, sub):
    nhp = B_HEADS // 2
    rows = seq_per_step * t_len
    kern = functools.partial(_hgrn_kernel, n_seq=seq_per_step, t_len=t_len, chunk=chunk, sub=sub)
    col0 = 3 * nhp
    blk = lambda off: pl.BlockSpec((rows, LANES), lambda s, hp: (s, col0 + off + hp))
    vec = pl.BlockSpec((1, LANES), lambda s, hp: (0, hp))
    state = pl.BlockSpec((seq_per_step, 2, B_DK, B_DK), lambda s, hp: (s, hp, 0, 0))
    return pl.pallas_call(
        kern,
        grid=(n_seq // seq_per_step, nhp),
        in_specs=[blk(0), blk(nhp), blk(2 * nhp), blk(3 * nhp), vec, vec, state],
        out_specs=[pl.BlockSpec((rows, LANES), lambda s, hp: (s, hp)), state],
        out_shape=[jax.ShapeDtypeStruct((n_seq * t_len, B_HEADS * B_DK), F32),
                   jax.ShapeDtypeStruct((n_seq, B_HEADS, B_DK, B_DK), F32)],
        compiler_params=_cparams("parallel", "parallel"),
        name="hgrn",
    )(proj, proj, proj, proj, lb.reshape(1, -1), jnp.tile(gnorm, B_HEADS).reshape(1, -1), s0)


def _diff_attn_prompt_kernel(q_ref, k_ref, v_ref, slope_ref, lam_ref, g_ref, o_ref,
                             q2_ref, m_ref, l_ref, acc_ref, *, tq, tk, post_scale):
    qi = pl.program_id(2)
    lo = lax.broadcasted_iota(jnp.int32, (tq, LANES), 1) < C_HD
    q = q_ref[...] * (C_HD ** -0.5)
    q2_ref[...] = jnp.concatenate([jnp.where(lo, q, 0.0), jnp.where(lo, 0.0, q)], axis=0).astype(BF16)
    m_ref[...] = jnp.full(m_ref.shape, NEG, F32)
    l_ref[...] = jnp.zeros(l_ref.shape, F32)
    acc_ref[...] = jnp.zeros(acc_ref.shape, F32)
    slope = slope_ref[0:1, 0:1]
    base = (lax.broadcasted_iota(jnp.int32, (tq, tk), 0) - lax.broadcasted_iota(jnp.int32, (tq, tk), 1)).astype(F32)

    def body(j, carry):
        rows = pl.ds(pl.multiple_of(j * tk, tk), tk)
        kb = k_ref[rows, :].astype(BF16)
        vb = v_ref[rows, :].astype(BF16)
        s = _dot_nt(q2_ref[...], kb)
        dist = base + (qi * tq - j * tk).astype(F32)
        bias = jnp.where(dist >= 0, -slope * dist, NEG)
        for mp in range(2):
            sm = s[mp * tq:(mp + 1) * tq] + bias
            m_prev = m_ref[mp]
            m_next = jnp.maximum(m_prev, jnp.max(sm, axis=1, keepdims=True))
            alpha = jnp.exp(m_prev - m_next)
            p = jnp.exp(sm - jnp.concatenate([m_next] * (tk // LANES), axis=1))
            l_ref[mp] = alpha * l_ref[mp] + jnp.sum(p, axis=1, keepdims=True)
            acc_ref[mp] = alpha * acc_ref[mp] + _dot(p.astype(BF16), vb)
            m_ref[mp] = m_next
        return carry

    lax.fori_loop(0, ((qi + 1) * tq) // tk, body, 0)
    o = acc_ref[0] / l_ref[0] - lam_ref[0:1, 0:1] * (acc_ref[1] / l_ref[1])
    o_ref[...] = _rms(o, g_ref[...]) * post_scale


def diff_attn_prompt(proj, slopes, lam, subln, post_scale, batch, seq_len, *, tq, tk):
    assert tq % tk == 0
    nq = seq_len // tq
    kern = functools.partial(_diff_attn_prompt_kernel, tq=tq, tk=tk, post_scale=post_scale)
    kv = lambda off: pl.BlockSpec((seq_len, LANES), lambda b, h, i: (b, off + h))
    stat = pltpu.VMEM((2, tq, LANES), F32)
    return pl.pallas_call(
        kern,
        grid=(batch, C_HEADS, nq),
        in_specs=[pl.BlockSpec((tq, LANES), lambda b, h, i: (b * nq + i, h)),
                  kv(C_HEADS), kv(2 * C_HEADS),
                  pl.BlockSpec((None, 1, LANES), lambda b, h, i: (h, 0, 0)),
                  pl.BlockSpec((1, LANES), lambda b, h, i: (0, 0)),
                  pl.BlockSpec((1, LANES), lambda b, h, i: (0, 0))],
        out_specs=pl.BlockSpec((tq, LANES), lambda b, h, i: (b * nq + i, h)),
        out_shape=jax.ShapeDtypeStruct((batch * seq_len, C_HEADS * 2 * C_HD), F32),
        scratch_shapes=[pltpu.VMEM((2 * tq, LANES), BF16), stat, stat, stat],
        compiler_params=_cparams("parallel", "parallel", "arbitrary"),
        name="diff_attn_prompt",
    )(proj, proj, proj, slopes, lam, subln.reshape(1, -1))


def _diff_attn_sample_kernel(table_ref, q_ref, kn_ref, vn_ref, *rest, n_pages, t_new, post_scale):
    k_pages = rest[:n_pages]
    v_pages = rest[n_pages:2 * n_pages]
    bias_ref, lam_ref, g_ref, o_ref, kx_ref, vx_ref = rest[2 * n_pages:]
    del table_ref
    width = C_HEADS * 2 * C_HD
    n_past = n_pages * PAGE_SIZE
    n_ext = kx_ref.shape[0]
    for p in range(n_pages):
        kx_ref[p * PAGE_SIZE:(p + 1) * PAGE_SIZE, :] = k_pages[p][...].astype(BF16)
        vx_ref[p * PAGE_SIZE:(p + 1) * PAGE_SIZE, :] = v_pages[p][...].astype(BF16)
    zpad = jnp.zeros((16 - t_new, width), F32)
    kx_ref[n_past:n_past + 16, :] = jnp.concatenate([kn_ref[...], zpad], axis=0).astype(BF16)
    vx_ref[n_past:n_past + 16, :] = jnp.concatenate([vn_ref[...], zpad], axis=0).astype(BF16)
    kx_ref[n_past + 16:, :] = jnp.zeros((n_ext - n_past - 16, width), BF16)
    vx_ref[n_past + 16:, :] = jnp.zeros((n_ext - n_past - 16, width), BF16)

    half = C_HEADS * t_new
    rmap = lax.broadcasted_iota(jnp.int32, (2 * half, width), 0) // t_new
    lmap = lax.broadcasted_iota(jnp.int32, (2 * half, width), 1) // C_HD
    own = ((rmap % C_HEADS) * 2 + rmap // C_HEADS) == lmap
    q = q_ref[...] * (C_HD ** -0.5)
    qs = jnp.where(own, jnp.concatenate([q] * (2 * C_HEADS), axis=0), 0.0).astype(BF16)
    s = _dot_nt(qs, kx_ref[...]) + bias_ref[...]
    m = jnp.max(s, axis=1, keepdims=True)
    p = jnp.exp(s - m)
    pn = p / jnp.sum(p, axis=1, keepdims=True)
    a = pn[:half] - lam_ref[0:1, 0:1] * pn[half:]
    o = _dot(a.astype(BF16), vx_ref[...])
    rhead = lax.broadcasted_iota(jnp.int32, (half, width), 0) // t_new
    lhead = lax.broadcasted_iota(jnp.int32, (half, width), 1) // (2 * C_HD)
    o = jnp.sum(jnp.where(rhead == lhead, o, 0.0).reshape(C_HEADS, t_new, width), axis=0)
    g = g_ref[...]
    outs = [_rms(o[:, h * LANES:(h + 1) * LANES], g) for h in range(C_HEADS)]
    o_ref[...] = jnp.concatenate(outs, axis=1) * post_scale


def diff_attn_sample(proj, cache_k, cache_v, table, lam, subln, post_scale, n_seq, t_new):
    width = C_HEADS * 2 * C_HD
    n_pages = table.shape[1]
    n_past = n_pages * PAGE_SIZE
    n_ext = n_past + LANES
    slopes = _alibi_slopes(C_HEADS)
    i = np.arange(t_new)[:, None]
    row = np.arange(n_ext)[None, :]
    dist = n_past + i - row
    ok = (dist >= 0) & (row < n_past + t_new)
    bias = np.stack([np.where(ok, -slopes[h] * dist, NEG) for h in range(C_HEADS)]).reshape(C_HEADS * t_new, n_ext)
    bias = jnp.asarray(np.concatenate([bias, bias], axis=0), F32)
    kern = functools.partial(_diff_attn_sample_kernel, n_pages=n_pages, t_new=t_new, post_scale=post_scale)
    new = lambda col: pl.BlockSpec((t_new, width), lambda s, tbl: (s, col))
    page = lambda p: pl.BlockSpec((None, PAGE_SIZE, width), lambda s, tbl, p=p: (tbl[s, p], 0, 0))
    pages = [page(p) for p in range(n_pages)]
    return pl.pallas_call(
        kern,
        grid_spec=pltpu.PrefetchScalarGridSpec(
            num_scalar_prefetch=1,
            grid=(n_seq,),
            in_specs=[new(0), new(1), new(2)] + pages + pages + [
                pl.BlockSpec((2 * C_HEADS * t_new, n_ext), lambda s, tbl: (0, 0)),
                pl.BlockSpec((1, LANES), lambda s, tbl: (0, 0)),
                pl.BlockSpec((1, LANES), lambda s, tbl: (0, 0))],
            out_specs=pl.BlockSpec((t_new, width), lambda s, tbl: (s, 0)),
            scratch_shapes=[pltpu.VMEM((n_ext, width), BF16), pltpu.VMEM((n_ext, width), BF16)]),
        out_shape=jax.ShapeDtypeStruct((n_seq * t_new, width), F32),
        compiler_params=_cparams("arbitrary"),
        name="diff_attn_sample",
    )(table, proj, proj, proj, *([cache_k] * n_pages), *([cache_v] * n_pages), bias, lam, subln.reshape(1, -1))


S5_BLOCKS = 4
S5_BLOCK_STATES = (S5_GROUPS // S5_BLOCKS) * S5_STATE
S5_SEQ_ROWS = 8


def _gelu_tanh(x):
    return x * (0.5 * (1.0 + jnp.tanh(math.sqrt(2.0 / math.pi) * (x + 0.044715 * (x * x * x)))))


def _s5_kernel(u_ref, h0_ref, a_ref, bm_ref, cm_ref, d_ref, wglu_ref, bglu_ref, o_ref, hout_ref,
               utm_ref, hb_ref, otm_ref, hst_ref, *, n_seq, tb):
    t = pl.program_id(1)
    rows = S5_SEQ_ROWS * tb
    ns = S5_BLOCK_STATES

    @pl.when(t == 0)
    def _():
        hst_ref[...] = h0_ref[...]

    if n_seq < S5_SEQ_ROWS:
        utm_ref[...] = jnp.zeros(utm_ref.shape, F32)
    for k in range(S5_BLOCKS):
        for j in range(n_seq):
            utm_ref[k, pl.ds(j, tb, stride=S5_SEQ_ROWS), :] = u_ref[j, :, k * LANES:(k + 1) * LANES]
        hb_ref[k] = _dot(utm_ref[k].astype(BF16), bm_ref[k].astype(BF16))

    def step(i, hs):
        sel = pl.ds(pl.multiple_of(i * S5_SEQ_ROWS, S5_SEQ_ROWS), S5_SEQ_ROWS)
        new = []
        for k in range(S5_BLOCKS):
            h_re, h_im = hs[k]
            a_re = a_ref[k, 0:1, 0:ns]
            a_im = a_ref[k, 1:2, 0:ns]
            n_re = a_re * h_re - a_im * h_im + hb_ref[k, sel, 0:ns]
            n_im = a_re * h_im + a_im * h_re + hb_ref[k, sel, ns:2 * ns]
            hb_ref[k, sel, 0:ns] = n_re
            hb_ref[k, sel, ns:2 * ns] = n_im
            new.append((n_re, n_im))
        return tuple(new)

    hs = tuple((hst_ref[k, :, 0:ns], hst_ref[k, :, ns:2 * ns]) for k in range(S5_BLOCKS))
    hs = lax.fori_loop(0, tb, step, hs)
    for k in range(S5_BLOCKS):
        hst_ref[k, :, 0:ns] = hs[k][0]
        hst_ref[k, :, ns:2 * ns] = hs[k][1]

    ys = [_dot(hb_ref[k].astype(BF16), cm_ref[k].astype(BF16)) + d_ref[k] * utm_ref[k] for k in range(S5_BLOCKS)]
    z = _gelu_tanh(jnp.concatenate(ys, axis=1))
    gate = jax.nn.sigmoid(_dot(z.astype(BF16), wglu_ref[...].astype(BF16)) + bglu_ref[...])
    od = z * gate
    for k in range(S5_BLOCKS):
        otm_ref[k] = od[:, k * LANES:(k + 1) * LANES]
    for j in range(n_seq):
        for k in range(S5_BLOCKS):
            o_ref[j, :, k * LANES:(k + 1) * LANES] = otm_ref[k, pl.ds(j, tb, stride=S5_SEQ_ROWS), :]

    @pl.when(t == pl.num_programs(1) - 1)
    def _():
        hout_ref[...] = hst_ref[...]


def _s5_params(a_re, a_im, log_dt, b_re, b_im, c_re, c_im, d_skip):
    dt = jnp.exp(log_dt)[:, None]
    mag = jnp.exp(a_re * dt)
    ab_re, ab_im = mag * jnp.cos(a_im * dt), mag * jnp.sin(a_im * dt)
    den = a_re * a_re + a_im * a_im
    xr, xi = ab_re - 1.0, ab_im
    z_re = (xr * a_re + xi * a_im) / den
    z_im = (xi * a_re - xr * a_im) / den
    bb_re = z_re[..., None] * b_re - z_im[..., None] * b_im
    bb_im = z_re[..., None] * b_im + z_im[..., None] * b_re
    gb = S5_GROUPS // S5_BLOCKS
    eye = jnp.eye(gb, dtype=F32)

    def in_mat(bb):
        bb = bb.reshape(S5_BLOCKS, gb, S5_STATE, S5_GROUP_CH)
        return jnp.einsum('kgpc,gh->kgchp', bb, eye).reshape(S5_BLOCKS, gb * S5_GROUP_CH, gb * S5_STATE)

    def out_mat(cc):
        cc = cc.reshape(S5_BLOCKS, gb, S5_GROUP_CH, S5_STATE)
        return jnp.einsum('kgcp,gh->khpgc', cc, eye).reshape(S5_BLOCKS, gb * S5_STATE, gb * S5_GROUP_CH)

    bm = jnp.concatenate([in_mat(bb_re), in_mat(bb_im)], axis=2)
    cm = jnp.concatenate([out_mat(c_re), out_mat(-c_im)], axis=1)

    def lanes(x):
        x = x.reshape(S5_BLOCKS, gb * S5_STATE)
        return jnp.concatenate([x, x], axis=1)

    a = jnp.stack([lanes(ab_re), lanes(ab_im)], axis=1)
    dvec = d_skip.reshape(S5_BLOCKS, 1, gb * S5_GROUP_CH)
    return a, bm, cm, dvec


def _s5_state_to_blocks(h_re, h_im, n_rows):
    n = h_re.shape[0]
    gb = S5_GROUPS // S5_BLOCKS
    r = h_re.reshape(n, S5_BLOCKS, gb * S5_STATE)
    i = h_im.reshape(n, S5_BLOCKS, gb * S5_STATE)
    h = jnp.transpose(jnp.concatenate([r, i], axis=2), (1, 0, 2))
    if n_rows > n:
        h = jnp.concatenate([h, jnp.zeros((S5_BLOCKS, n_rows - n, h.shape[2]), F32)], axis=1)
    return h


def _s5_blocks_to_state(h, n):
    gb = S5_GROUPS // S5_BLOCKS
    h = jnp.transpose(h[:, :n], (1, 0, 2))
    re = h[:, :, :gb * S5_STATE].reshape(n, S5_GROUPS, S5_STATE)
    im = h[:, :, gb * S5_STATE:].reshape(n, S5_GROUPS, S5_STATE)
    return re, im


def s5(proj3, h0_blocks, params, wglu, bglu, *, seq_per_step, tb):
    n_seq, t_len, _ = proj3.shape
    a, bm, cm, dvec = params
    width = S5_GROUPS * S5_GROUP_CH
    ucol = proj3.shape[2] // width - 1
    n_groups = n_seq // seq_per_step
    rows = S5_SEQ_ROWS * tb
    kern = functools.partial(_s5_kernel, n_seq=seq_per_step, tb=tb)
    full = lambda shape: pl.BlockSpec(shape, lambda s, t: (0,) * len(shape))
    state = pl.BlockSpec((S5_BLOCKS, S5_SEQ_ROWS, 2 * S5_BLOCK_STATES), lambda s, t: (0, s, 0))
    return pl.pallas_call(
        kern,
        grid=(n_groups, t_len // tb),
        in_specs=[pl.BlockSpec((seq_per_step, tb, width), lambda s, t: (s, t, ucol)),
                  state, full(a.shape), full(bm.shape), full(cm.shape), full(dvec.shape),
                  full(wglu.shape), full((1, width))],
        out_specs=[pl.BlockSpec((seq_per_step, tb, width), lambda s, t: (s, t, 0)), state],
        out_shape=[jax.ShapeDtypeStruct((n_seq, t_len, width), F32),
                   jax.ShapeDtypeStruct(h0_blocks.shape, F32)],
        scratch_shapes=[pltpu.VMEM((S5_BLOCKS, rows, LANES), F32),
                        pltpu.VMEM((S5_BLOCKS, rows, 2 * S5_BLOCK_STATES), F32),
                        pltpu.VMEM((S5_BLOCKS, rows, LANES), F32),
                        pltpu.VMEM((S5_BLOCKS, S5_SEQ_ROWS, 2 * S5_BLOCK_STATES), F32)],
        compiler_params=_cparams("parallel", "arbitrary"),
        name="s5",
    )(proj3, h0_blocks, a, bm, cm, dvec, wglu, bglu.reshape(1, width))


def _trunk(x, p, past, *, n_seq, t_len):
    n = n_seq * t_len
    tm = min(n, 1024)
    prompt = past is None
    a_width = A_HEADS * HEAD_DIM
    c_width = C_HEADS * 2 * C_HD
    new = {}

    proj = rms_matmul(x, p['norm_mix'][0], p['w_in_even'][0], tm=tm, tn=512)
    ka = proj[:, a_width:2 * a_width].reshape(n_seq, t_len, A_HEADS, HEAD_DIM)
    va = proj[:, 2 * a_width:3 * a_width].reshape(n_seq, t_len, A_HEADS, HEAD_DIM)
    lb = jnp.cumsum(jax.nn.softmax(p['hgrn_lb'].astype(F32), axis=0), axis=0)[0]
    if prompt:
        keep = min(A_MAX_WINDOW, t_len)
        new['a_k'], new['a_v'] = ka[:, t_len - keep:], va[:, t_len - keep:]
        o_a = attn_a_prompt(proj, n_seq, t_len)
        s0 = jnp.zeros((n_seq, B_HEADS, B_DK, B_DK), F32)
        o_b, s_fin = hgrn(proj, lb, p['hgrn_gnorm'][0], s0, n_seq=n_seq, t_len=t_len,
                          seq_per_step=1, chunk=64, sub=16)
    else:
        new['a_k'], new['a_v'] = ka, va
        n_cache = past['a_k'].shape[2]
        o_a = attn_a_sample(proj, past['a_k'][0].reshape(n_seq, n_cache, a_width),
                            past['a_v'][0].reshape(n_seq, n_cache, a_width), n_seq, t_len)
        o_b, s_fin = hgrn(proj, lb, p['hgrn_gnorm'][0], past['hgrn'][0], n_seq=n_seq, t_len=t_len,
                          seq_per_step=16, chunk=16, sub=16)
    new['hgrn'] = s_fin
    x = mix_out(x, o_a, o_b, p['w_out_even'][0], tm=tm)
    x = ffn(x, p['norm_ffn'][0], p['ffn_w_gate'][0], p['ffn_w_up'][0], p['ffn_w_down'][0], tm=tm, tf=256)

    proj = rms_matmul(x, p['norm_mix'][1], p['w_in_odd'][0], tm=tm, tn=512)
    new['c_k'] = proj[:, c_width:2 * c_width].reshape(n_seq, t_len, C_HEADS, 2 * C_HD)
    new['c_v'] = proj[:, 2 * c_width:3 * c_width].reshape(n_seq, t_len, C_HEADS, 2 * C_HD)
    lam_init = 0.8 - 0.6 * math.exp(-0.3 * 1)
    lam = (jnp.exp(jnp.sum(p['diff_lq1'][0] * p['diff_lk1'][0]))
           - jnp.exp(jnp.sum(p['diff_lq2'][0] * p['diff_lk2'][0])) + lam_init)
    lam = jnp.full((1, LANES), lam, F32)
    s5_params = _s5_params(p['s5_a_re'][0], p['s5_a_im'][0], p['s5_log_dt'][0], p['s5_b_re'][0], p['s5_b_im'][0],
                           p['s5_c_re'][0], p['s5_c_im'][0], p['s5_d'][0])
    proj3 = proj.reshape(n_seq, t_len, proj.shape[1])
    if prompt:
        slopes = jnp.asarray(np.broadcast_to(_alibi_slopes(C_HEADS)[:, None, None], (C_HEADS, 1, LANES)), F32)
        o_c = diff_attn_prompt(proj, slopes, lam, p['diff_subln'][0], 1.0 - lam_init, n_seq, t_len, tq=256, tk=256)
        h0 = jnp.zeros((S5_BLOCKS, S5_SEQ_ROWS, 2 * S5_BLOCK_STATES), F32)
        o_d, h_fin = s5(proj3, h0, s5_params, p['s5_w_glu'][0], p['s5_b_glu'][0], seq_per_step=n_seq, tb=128)
    else:
        n_phys = past['c_k'].shape[1]
        o_c = diff_attn_sample(proj, past['c_k'][0].reshape(n_phys, PAGE_SIZE, c_width),
                               past['c_v'][0].reshape(n_phys, PAGE_SIZE, c_width),
                               past['page_table'], lam, p['diff_subln'][0], 1.0 - lam_init, n_seq, t_len)
        h0 = _s5_state_to_blocks(past['s5_re'][0], past['s5_im'][0], n_seq)
        o_d, h_fin = s5(proj3, h0, s5_params, p['s5_w_glu'][0], p['s5_b_glu'][0],
                        seq_per_step=S5_SEQ_ROWS, tb=t_len)
    new['s5_re'], new['s5_im'] = _s5_blocks_to_state(h_fin, n_seq)
    x = mix_out(x, o_c, o_d.reshape(n, -1), p['w_out_odd'][0], tm=tm)
    x = moe(x, p['norm_ffn'][1], p['moe_router_w'][0], p['moe_router_b'][0],
            p['moe_w_gate'][0], p['moe_w_up'][0], p['moe_w_down'][0], tm=tm, tf=256)
    y = rmsnorm(x, p['norm_final'], tm=tm)
    return y.reshape(n_seq, t_len, D_MODEL), {k: v[None] for k, v in new.items()}


def kernel(x_prompt, x_sample, cache_a_k, cache_a_v, state_hgrn, cache_c_k, cache_c_v, state_s5_re, state_s5_im,
           page_table, norm_mix, norm_ffn, norm_final, w_in_even, w_out_even, hgrn_lb, hgrn_gnorm,
           ffn_w_gate, ffn_w_up, ffn_w_down, w_in_odd, w_out_odd, diff_lq1, diff_lk1, diff_lq2, diff_lk2,
           diff_subln, s5_a_re, s5_a_im, s5_log_dt, s5_b_re, s5_b_im, s5_c_re, s5_c_im, s5_d, s5_w_glu,
           s5_b_glu, moe_router_w, moe_router_b, moe_w_gate, moe_w_up, moe_w_down):
    p = {
        'norm_mix': norm_mix, 'norm_ffn': norm_ffn, 'norm_final': norm_final,
        'w_in_even': w_in_even, 'w_out_even': w_out_even, 'hgrn_lb': hgrn_lb, 'hgrn_gnorm': hgrn_gnorm,
        'ffn_w_gate': ffn_w_gate, 'ffn_w_up': ffn_w_up, 'ffn_w_down': ffn_w_down,
        'w_in_odd': w_in_odd, 'w_out_odd': w_out_odd, 'diff_lq1': diff_lq1, 'diff_lk1': diff_lk1,
        'diff_lq2': diff_lq2, 'diff_lk2': diff_lk2, 'diff_subln': diff_subln,
        's5_a_re': s5_a_re, 's5_a_im': s5_a_im, 's5_log_dt': s5_log_dt, 's5_b_re': s5_b_re, 's5_b_im': s5_b_im,
        's5_c_re': s5_c_re, 's5_c_im': s5_c_im, 's5_d': s5_d, 's5_w_glu': s5_w_glu, 's5_b_glu': s5_b_glu,
        'moe_router_w': moe_router_w, 'moe_router_b': moe_router_b,
        'moe_w_gate': moe_w_gate, 'moe_w_up': moe_w_up, 'moe_w_down': moe_w_down,
    }
    past = {'a_k': cache_a_k, 'a_v': cache_a_v, 'hgrn': state_hgrn, 'c_k': cache_c_k, 'c_v': cache_c_v,
            's5_re': state_s5_re, 's5_im': state_s5_im, 'page_table': page_table}
    bp, tp, _ = x_prompt.shape
    bs, ts, _ = x_sample.shape
    y_prompt, sp = _trunk(x_prompt.reshape(bp * tp, D_MODEL), p, None, n_seq=bp, t_len=tp)
    y_sample, ss = _trunk(x_sample.reshape(bs * ts, D_MODEL), p, past, n_seq=bs, t_len=ts)
    names = ('a_k', 'a_v', 'hgrn', 'c_k', 'c_v', 's5_re', 's5_im')
    return (y_prompt, y_sample) + tuple(sp[k] for k in names) + tuple(ss[k] for k in names)
```

```python
import functools
import math

import jax
import jax.numpy as jnp
import numpy as np
from jax import lax
from jax.experimental import pallas as pl
from jax.experimental.pallas import tpu as pltpu

F32 = jnp.float32
BF16 = jnp.bfloat16

D_MODEL = 1024
HEAD_DIM = 64
A_HEADS = 8
A_GROUPS = ((128, 1), (512, 4), (2048, 16))
A_MAX_WINDOW = 2048
A_TILE = 128
B_HEADS = 8
B_DK = 64
C_HEADS = 4
C_HD = 64
S5_GROUPS = 32
S5_GROUP_CH = 16
S5_STATE = 64
N_EXPERTS = 8
PAGE_SIZE = 128
RMS_EPS = 1e-6
NEG = -1e30
LOG2E = math.log2(math.e)

LANES = 128
VMEM_LIMIT = 52 * 1024 * 1024


def _cparams(*sem):
    return pltpu.CompilerParams(dimension_semantics=sem, vmem_limit_bytes=VMEM_LIMIT)


def _dot(a, b):
    return jnp.dot(a, b, preferred_element_type=F32)


def _dot_nt(a, b):
    return lax.dot_general(a, b, (((1,), (1,)), ((), ())), preferred_element_type=F32)


def _dot_tn(a, b):
    return lax.dot_general(a, b, (((0,), (0,)), ((), ())), preferred_element_type=F32)


def _rms(x, g):
    return x * lax.rsqrt(jnp.mean(x * x, axis=-1, keepdims=True) + RMS_EPS) * g


def _alibi_slopes(n):
    return 2.0 ** (-8.0 * np.arange(1, n + 1) / n)


def _rms_matmul_kernel(x_ref, g_ref, w_ref, o_ref, xn_ref):
    @pl.when(pl.program_id(1) == 0)
    def _():
        xn_ref[...] = _rms(x_ref[...], g_ref[...]).astype(BF16)

    o_ref[...] = _dot(xn_ref[...], w_ref[...].astype(BF16))


def rms_matmul(x, g, w, *, tm, tn):
    n, d = x.shape
    nout = w.shape[1]
    return pl.pallas_call(
        _rms_matmul_kernel,
        grid=(n // tm, nout // tn),
        in_specs=[pl.BlockSpec((tm, d), lambda i, j: (i, 0)),
                  pl.BlockSpec((1, d), lambda i, j: (0, 0)),
                  pl.BlockSpec((d, tn), lambda i, j: (0, j))],
        out_specs=pl.BlockSpec((tm, tn), lambda i, j: (i, j)),
        out_shape=jax.ShapeDtypeStruct((n, nout), F32),
        scratch_shapes=[pltpu.VMEM((tm, d), BF16)],
        compiler_params=_cparams("parallel", "arbitrary"),
        name="rms_matmul",
    )(x, g.reshape(1, d), w)


def _mix_out_kernel(res_ref, a_ref, b_ref, wa_ref, wb_ref, o_ref):
    acc = _dot(a_ref[...].astype(BF16), wa_ref[...].astype(BF16))
    acc = acc + _dot(b_ref[...].astype(BF16), wb_ref[...].astype(BF16))
    o_ref[...] = res_ref[...] + acc


def mix_out(res, a, b, w, *, tm):
    n, d = res.shape
    ka = a.shape[1]
    return pl.pallas_call(
        _mix_out_kernel,
        grid=(n // tm,),
        in_specs=[pl.BlockSpec((tm, d), lambda i: (i, 0)),
                  pl.BlockSpec((tm, ka), lambda i: (i, 0)),
                  pl.BlockSpec((tm, ka), lambda i: (i, 0)),
                  pl.BlockSpec((ka, d), lambda i: (0, 0)),
                  pl.BlockSpec((ka, d), lambda i: (1, 0))],
        out_specs=pl.BlockSpec((tm, d), lambda i: (i, 0)),
        out_shape=jax.ShapeDtypeStruct((n, d), F32),
        compiler_params=_cparams("parallel"),
        name="mix_out",
    )(res, a, b, w, w)


def _ffn_kernel(x_ref, g_ref, wg_ref, wu_ref, wd_ref, o_ref, xn_ref, acc_ref):
    f = pl.program_id(1)

    @pl.when(f == 0)
    def _():
        xn_ref[...] = _rms(x_ref[...], g_ref[...]).astype(BF16)
        acc_ref[...] = jnp.zeros_like(acc_ref)

    xn = xn_ref[...]
    hg = _dot(xn, wg_ref[...].astype(BF16))
    hu = _dot(xn, wu_ref[...].astype(BF16))
    h = (jax.nn.silu(hg) * hu).astype(BF16)
    acc_ref[...] += _dot(h, wd_ref[...].astype(BF16))

    @pl.when(f == pl.num_programs(1) - 1)
    def _():
        o_ref[...] = x_ref[...] + acc_ref[...]


def ffn(x, g, wg, wu, wd, *, tm, tf):
    n, d = x.shape
    dff = wg.shape[1]
    return pl.pallas_call(
        _ffn_kernel,
        grid=(n // tm, dff // tf),
        in_specs=[pl.BlockSpec((tm, d), lambda i, f: (i, 0)),
                  pl.BlockSpec((1, d), lambda i, f: (0, 0)),
                  pl.BlockSpec((d, tf), lambda i, f: (0, f)),
                  pl.BlockSpec((d, tf), lambda i, f: (0, f)),
                  pl.BlockSpec((tf, d), lambda i, f: (f, 0))],
        out_specs=pl.BlockSpec((tm, d), lambda i, f: (i, 0)),
        out_shape=jax.ShapeDtypeStruct((n, d), F32),
        scratch_shapes=[pltpu.VMEM((tm, d), BF16), pltpu.VMEM((tm, d), F32)],
        compiler_params=_cparams("parallel", "arbitrary"),
        name="ffn",
    )(x, g.reshape(1, d), wg, wu, wd)


def _moe_kernel(x_ref, g_ref, rw_ref, rb_ref, wg_ref, wu_ref, wd_ref, o_ref, xn_ref, acc_ref, comb_ref):
    e = pl.program_id(1)
    f = pl.program_id(2)
    tm = x_ref.shape[0]
    lane = lax.broadcasted_iota(jnp.int32, (tm, LANES), 1)

    @pl.when((e == 0) & (f == 0))
    def _():
        hn = _rms(x_ref[...], g_ref[...])
        xn_ref[...] = hn.astype(BF16)
        acc_ref[...] = jnp.zeros_like(acc_ref)
        logits = jnp.dot(hn, rw_ref[...], precision=lax.Precision.HIGHEST,
                         preferred_element_type=F32) + rb_ref[...]
        lg = jnp.where(lane < N_EXPERTS, logits, NEG)
        m1 = jnp.max(lg, axis=1, keepdims=True)
        i1 = jnp.min(jnp.where(lg == m1, lane, LANES), axis=1, keepdims=True)
        lg2 = jnp.where(lane == i1, NEG, lg)
        m2 = jnp.max(lg2, axis=1, keepdims=True)
        i2 = jnp.min(jnp.where(lg2 == m2, lane, LANES), axis=1, keepdims=True)
        e2 = jnp.exp(m2 - m1)
        den = 1.0 + e2
        comb_ref[...] = jnp.where(lane == i1, 1.0 / den, 0.0) + jnp.where(lane == i2, e2 / den, 0.0)

    xn = xn_ref[...]
    hg = _dot(xn, wg_ref[...].astype(BF16))
    hu = _dot(xn, wu_ref[...].astype(BF16))
    h = (jax.nn.silu(hg) * hu).astype(BF16)
    y = _dot(h, wd_ref[...].astype(BF16))
    c = jnp.sum(jnp.where(lane == e, comb_ref[...], 0.0), axis=1, keepdims=True)
    acc_ref[...] += c * y

    @pl.when((e == pl.num_programs(1) - 1) & (f == pl.num_programs(2) - 1))
    def _():
        o_ref[...] = x_ref[...] + acc_ref[...]


def moe(x, g, rw, rb, wg, wu, wd, *, tm, tf):
    n, d = x.shape
    ne, _, dff = wg.shape
    rw_pad = jnp.zeros((d, LANES), F32).at[:, :ne].set(rw)
    rb_pad = jnp.zeros((1, LANES), F32).at[0, :ne].set(rb)
    return pl.pallas_call(
        _moe_kernel,
        grid=(n // tm, ne, dff // tf),
        in_specs=[pl.BlockSpec((tm, d), lambda i, e, f: (i, 0)),
                  pl.BlockSpec((1, d), lambda i, e, f: (0, 0)),
                  pl.BlockSpec((d, LANES), lambda i, e, f: (0, 0)),
                  pl.BlockSpec((1, LANES), lambda i, e, f: (0, 0)),
                  pl.BlockSpec((None, d, tf), lambda i, e, f: (e, 0, f)),
                  pl.BlockSpec((None, d, tf), lambda i, e, f: (e, 0, f)),
                  pl.BlockSpec((None, tf, d), lambda i, e, f: (e, f, 0))],
        out_specs=pl.BlockSpec((tm, d), lambda i, e, f: (i, 0)),
        out_shape=jax.ShapeDtypeStruct((n, d), F32),
        scratch_shapes=[pltpu.VMEM((tm, d), BF16), pltpu.VMEM((tm, d), F32), pltpu.VMEM((tm, LANES), F32)],
        compiler_params=_cparams("parallel", "arbitrary", "arbitrary"),
        name="moe",
    )(x, g.reshape(1, d), rw_pad, rb_pad, wg, wu, wd)


def _rmsnorm_kernel(x_ref, g_ref, o_ref):
    o_ref[...] = _rms(x_ref[...], g_ref[...])


def rmsnorm(x, g, *, tm):
    n, d = x.shape
    return pl.pallas_call(
        _rmsnorm_kernel,
        grid=(n // tm,),
        in_specs=[pl.BlockSpec((tm, d), lambda i: (i, 0)), pl.BlockSpec((1, d), lambda i: (0, 0))],
        out_specs=pl.BlockSpec((tm, d), lambda i: (i, 0)),
        out_shape=jax.ShapeDtypeStruct((n, d), F32),
        compiler_params=_cparams("parallel"),
        name="rmsnorm",
    )(x, g.reshape(1, d))


def _attn_a_prompt_kernel(q_ref, k_ref, v_ref, bias_ref, o_ref,
                          kx_ref, vx_ref, m0_ref, l0_ref, m1_ref, l1_ref, acc_ref, *, seq_len):
    pad = A_MAX_WINDOW
    kx_ref[0:pad, :] = jnp.zeros((pad, LANES), F32)
    vx_ref[0:pad, :] = jnp.zeros((pad, LANES), F32)
    kx_ref[pad:pad + seq_len, :] = k_ref[...]
    vx_ref[pad:pad + seq_len, :] = v_ref[...]
    lo = lax.broadcasted_iota(jnp.int32, (A_TILE, LANES), 1) < HEAD_DIM
    n_tiles = seq_len // A_TILE

    for g, (_, d) in enumerate(A_GROUPS):
        shift = int(math.log2(d))

        def tile(i, carry, g=g, d=d, shift=shift):
            m = lax.shift_right_logical(i, shift)
            r = i & (d - 1)
            row0 = r + (d * A_TILE) * m
            if d == 1:
                row0 = pl.multiple_of(row0, A_TILE)
                qsel = pl.ds(row0, A_TILE)
                ksel = pl.ds(row0 + pad - A_TILE, 2 * A_TILE)
            else:
                qsel = pl.ds(row0, A_TILE, stride=d)
                ksel = pl.ds(row0 + pad - A_TILE * d, 2 * A_TILE, stride=d)
            q = q_ref[qsel, :] * (HEAD_DIM ** -0.5)
            q2 = jnp.concatenate([jnp.where(lo, q, 0.0), jnp.where(lo, 0.0, q)], axis=0).astype(BF16)
            kb = kx_ref[ksel, :].astype(BF16)
            vb = vx_ref[ksel, :].astype(BF16)
            s = _dot_nt(q2, kb)
            first = jnp.where(m == 0, 0, 1)
            s0 = s[:A_TILE] + bias_ref[g, 0, first]
            s1 = s[A_TILE:] + bias_ref[g, 1, first]
            mt0 = jnp.max(s0, axis=1, keepdims=True)
            mt1 = jnp.max(s1, axis=1, keepdims=True)
            if g == 0:
                mn0 = jnp.broadcast_to(mt0, (A_TILE, LANES))
                mn1 = jnp.broadcast_to(mt1, (A_TILE, LANES))
            else:
                mo0 = m0_ref[qsel, :]
                mo1 = m1_ref[qsel, :]
                mn0 = jnp.maximum(mo0, mt0)
                mn1 = jnp.maximum(mo1, mt1)
            p0 = jnp.exp(s0 - jnp.concatenate([mn0, mn0], axis=1))
            p1 = jnp.exp(s1 - jnp.concatenate([mn1, mn1], axis=1))
            ls0 = jnp.sum(p0, axis=1, keepdims=True)
            ls1 = jnp.sum(p1, axis=1, keepdims=True)
            pv0 = _dot(p0.astype(BF16), vb)
            pv1 = _dot(p1.astype(BF16), vb)
            pv = jnp.where(lo, pv0, pv1)
            if g == 0:
                l0_ref[qsel, :] = jnp.broadcast_to(ls0, (A_TILE, LANES))
                l1_ref[qsel, :] = jnp.broadcast_to(ls1, (A_TILE, LANES))
                acc_ref[qsel, :] = pv
            else:
                a0 = jnp.exp(mo0 - mn0)
                a1 = jnp.exp(mo1 - mn1)
                l0_ref[qsel, :] = a0 * l0_ref[qsel, :] + ls0
                l1_ref[qsel, :] = a1 * l1_ref[qsel, :] + ls1
                acc_ref[qsel, :] = jnp.where(lo, a0, a1) * acc_ref[qsel, :] + pv
            m0_ref[qsel, :] = mn0
            m1_ref[qsel, :] = mn1
            return carry

        lax.fori_loop(0, n_tiles, tile, 0, unroll=2)

    lo_full = lax.broadcasted_iota(jnp.int32, (seq_len, LANES), 1) < HEAD_DIM
    o_ref[...] = acc_ref[...] / jnp.where(lo_full, l0_ref[...], l1_ref[...])


def _attn_a_prompt_bias():
    slopes = _alibi_slopes(A_HEADS)
    i = np.arange(A_TILE)[:, None]
    c = np.arange(2 * A_TILE)[None, :]
    delta = A_TILE + i - c
    band = (delta >= 0) & (delta <= A_TILE)
    out = np.zeros((A_HEADS // 2, len(A_GROUPS), 2, 2, A_TILE, 2 * A_TILE), np.float32)
    for hp in range(A_HEADS // 2):
        for g, (_, d) in enumerate(A_GROUPS):
            for hs in range(2):
                b = -slopes[2 * hp + hs] * d * delta
                out[hp, g, hs, 1] = np.where(band, b, NEG)
                out[hp, g, hs, 0] = np.where(band & (c >= A_TILE), b, NEG)
    return jnp.asarray(out)


def attn_a_prompt(proj, batch, seq_len):
    nhp = A_HEADS // 2
    kern = functools.partial(_attn_a_prompt_kernel, seq_len=seq_len)
    blk = lambda off: pl.BlockSpec((seq_len, LANES), lambda b, hp: (b, off + hp))
    stat = pltpu.VMEM((seq_len, LANES), F32)
    return pl.pallas_call(
        kern,
        grid=(batch, nhp),
        in_specs=[blk(0), blk(nhp), blk(2 * nhp),
                  pl.BlockSpec((None, len(A_GROUPS), 2, 2, A_TILE, 2 * A_TILE), lambda b, hp: (hp, 0, 0, 0, 0, 0))],
        out_specs=pl.BlockSpec((seq_len, LANES), lambda b, hp: (b, hp)),
        out_shape=jax.ShapeDtypeStruct((batch * seq_len, A_HEADS * HEAD_DIM), F32),
        scratch_shapes=[pltpu.VMEM((A_MAX_WINDOW + seq_len, LANES), F32),
                        pltpu.VMEM((A_MAX_WINDOW + seq_len, LANES), F32),
                        stat, stat, stat, stat, stat],
        compiler_params=_cparams("parallel", "parallel"),
        name="attn_a_prompt",
    )(proj, proj, proj, _attn_a_prompt_bias())


def _attn_a_sample_kernel(q_ref, kn_ref, vn_ref, kt_ref, vt_ref, bias_ref, mult_ref, o_ref, *, t_new, n_cache):
    q = q_ref[...] * (HEAD_DIM ** -0.5)
    kn = kn_ref[...]
    vn = vn_ref[...]
    zpad = jnp.zeros((LANES - t_new, HEAD_DIM), F32)
    heads = [slice(h * HEAD_DIM, (h + 1) * HEAD_DIM) for h in range(A_HEADS)]
    s = []
    for h, cols in enumerate(heads):
        qh = q[:, cols].astype(BF16)
        s_cache = _dot(qh, kt_ref[h].astype(BF16))
        s_new = _dot_nt(qh, jnp.concatenate([kn[:, cols], zpad], axis=0).astype(BF16))
        s.append(jnp.concatenate([s_cache, s_new], axis=1))
    s = jnp.concatenate(s, axis=0) + bias_ref[...]
    m = jnp.max(s, axis=1, keepdims=True)
    p = mult_ref[...] * jnp.exp(s - m)
    p = p / jnp.sum(p, axis=1, keepdims=True)
    outs = []
    for h, cols in enumerate(heads):
        ph = p[h * t_new:(h + 1) * t_new].astype(BF16)
        o_cache = _dot_nt(ph[:, :n_cache], vt_ref[h].astype(BF16))
        o_new = _dot(ph[:, n_cache:], jnp.concatenate([vn[:, cols], zpad], axis=0).astype(BF16))
        outs.append(o_cache + o_new)
    o_ref[...] = jnp.concatenate(outs, axis=1)


def _attn_a_sample_consts(t_new, n_cache, n_ext):
    slopes = _alibi_slopes(A_HEADS)
    i = np.arange(t_new)[:, None]
    row = np.arange(n_ext)[None, :]
    delta = n_cache + i - row
    mult = np.zeros((t_new, n_ext), np.float32)
    for window, d in A_GROUPS:
        mult += ((delta >= 0) & (delta % d == 0) & (delta <= window) & (row < n_cache + t_new)).astype(np.float32)
    bias = np.stack([np.where(mult > 0, -slopes[h] * delta, NEG) for h in range(A_HEADS)])
    mult = np.broadcast_to(mult[None], (A_HEADS, t_new, n_ext))
    return (jnp.asarray(bias.reshape(A_HEADS * t_new, n_ext), F32),
            jnp.asarray(mult.reshape(A_HEADS * t_new, n_ext), F32))


def attn_a_sample(proj, cache_k, cache_v, n_seq, t_new):
    width = A_HEADS * HEAD_DIM
    n_cache = cache_k.shape[3]
    assert n_cache == A_MAX_WINDOW, "the dilated-attention sample kernel expects a full window buffer"
    n_ext = n_cache + LANES
    bias, mult = _attn_a_sample_consts(t_new, n_cache, n_ext)
    kern = functools.partial(_attn_a_sample_kernel, t_new=t_new, n_cache=n_cache)
    new = lambda col: pl.BlockSpec((t_new, width), lambda s: (s, col))
    cache = pl.BlockSpec((None, A_HEADS, HEAD_DIM, n_cache), lambda s: (s, 0, 0, 0))
    const = pl.BlockSpec((A_HEADS * t_new, n_ext), lambda s: (0, 0))
    return pl.pallas_call(
        kern,
        grid=(n_seq,),
        in_specs=[new(0), new(1), new(2), cache, cache, const, const],
        out_specs=pl.BlockSpec((t_new, width), lambda s: (s, 0)),
        out_shape=jax.ShapeDtypeStruct((n_seq * t_new, width), F32),
        compiler_params=_cparams("parallel"),
        name="attn_a_sample",
    )(proj, proj, proj, cache_k, cache_v, bias, mult)


def _hgrn_chunk(qraw, fraw, iv, graw, st, lb, gn, *, chunk, sub, valid):
    lo = lax.broadcasted_iota(jnp.int32, (chunk, LANES), 1) < B_DK
    rowi = lax.broadcasted_iota(jnp.int32, (chunk, LANES), 0)
    f = lb + (1.0 - lb) * jax.nn.sigmoid(fraw)
    logf = jnp.log(f)
    kk = 1.0 - f
    if valid < chunk:
        logf = jnp.where(rowi < valid, logf, 0.0)
        kk = jnp.where(rowi < valid, kk, 0.0)
    qh = jax.nn.silu(qraw) * (B_DK ** -0.5)
    tri_r = lax.broadcasted_iota(jnp.int32, (chunk, chunk), 0)
    tri_c = lax.broadcasted_iota(jnp.int32, (chunk, chunk), 1)
    causal = tri_c <= tri_r
    cum = jnp.dot(causal.astype(F32), logf, precision=lax.Precision.HIGHEST, preferred_element_type=F32)
    last = cum[chunk - 1:chunk, :]

    o = _dot_nt((qh * jnp.exp(cum)).astype(BF16), st.astype(BF16))

    att0, att1 = [], []
    lo_sub = lax.broadcasted_iota(jnp.int32, (sub, LANES), 1) < B_DK
    for blk in range(chunk // sub):
        r0 = blk * sub
        n = r0 + sub
        c0 = cum[r0 - 1:r0, :] if blk > 0 else jnp.zeros((1, LANES), F32)
        qt = qh[r0:n] * jnp.exp(cum[r0:n] - c0)
        q2 = jnp.concatenate([jnp.where(lo_sub, qt, 0.0), jnp.where(lo_sub, 0.0, qt)], axis=0).astype(BF16)
        kt = jnp.where(rowi < n, kk * jnp.exp(jnp.where(rowi < n, c0 - cum, 0.0)), 0.0).astype(BF16)
        a = _dot_nt(q2, kt)
        att0.append(a[:sub])
        att1.append(a[sub:])
    att0 = jnp.where(causal, jnp.concatenate(att0, axis=0), 0.0).astype(BF16)
    att1 = jnp.where(causal, jnp.concatenate(att1, axis=0), 0.0).astype(BF16)
    o = o + _dot(att0, jnp.where(lo, iv, 0.0).astype(BF16)) + _dot(att1, jnp.where(lo, 0.0, iv).astype(BF16))

    kend = (kk * jnp.exp(last - cum)).astype(BF16)
    upd = _dot_tn(iv.astype(BF16), kend)
    same = ((lax.broadcasted_iota(jnp.int32, (LANES, LANES), 0) < B_DK)
            == (lax.broadcasted_iota(jnp.int32, (LANES, LANES), 1) < B_DK))
    st_new = st * jnp.exp(last) + jnp.where(same, upd, 0.0)

    o2 = o * o
    ms0 = jnp.sum(jnp.where(lo, o2, 0.0), axis=1, keepdims=True)
    ms1 = jnp.sum(jnp.where(lo, 0.0, o2), axis=1, keepdims=True)
    ms = jnp.where(lo, ms0, ms1) * (1.0 / B_DK)
    out = o * lax.rsqrt(ms + RMS_EPS) * gn * jax.nn.silu(graw)
    return out, st_new


def _hgrn_kernel(q_ref, f_ref, i_ref, g_ref, lb_ref, gn_ref, s0_ref, o_ref, sout_ref, st_ref,
                 *, n_seq, t_blk, chunk, sub):
    tb = pl.program_id(1)
    nhp = B_HEADS // 2
    valid = min(t_blk, chunk)
    n_chunks = max(t_blk // chunk, 1)
    zero = jnp.zeros((B_DK, B_DK), F32)
    pairs = [slice(hp * LANES, (hp + 1) * LANES) for hp in range(nhp)]

    def one_seq(sq, carry):
        @pl.when(tb == 0)
        def _():
            for hp in range(nhp):
                s_bd = jnp.concatenate([jnp.concatenate([s0_ref[sq, 2 * hp], zero], axis=1),
                                        jnp.concatenate([zero, s0_ref[sq, 2 * hp + 1]], axis=1)], axis=0)
                st_ref[sq, hp] = s_bd.T

        def one_chunk(c, carry):
            base = pl.multiple_of(sq * t_blk + c * valid, 8)
            rows = pl.ds(base, valid)
            for hp, cols in enumerate(pairs):
                args = [r[rows, cols] for r in (q_ref, f_ref, i_ref, g_ref)]
                if valid < chunk:
                    args = [jnp.concatenate([a, jnp.zeros((chunk - valid, LANES), F32)], axis=0) for a in args]
                out, st = _hgrn_chunk(*args, st_ref[sq, hp], lb_ref[:, cols], gn_ref[:, cols],
                                      chunk=chunk, sub=sub, valid=valid)
                o_ref[rows, cols] = out[:valid]
                st_ref[sq, hp] = st
            return carry

        lax.fori_loop(0, n_chunks, one_chunk, 0)

        @pl.when(tb == pl.num_programs(1) - 1)
        def _():
            for hp in range(nhp):
                s_fin = st_ref[sq, hp].T
                sout_ref[sq, 2 * hp] = s_fin[:B_DK, :B_DK]
                sout_ref[sq, 2 * hp + 1] = s_fin[B_DK:, B_DK:]
        return carry

    lax.fori_loop(0, n_seq, one_seq, 0)


def hgrn(proj, lb, gnorm, s0, *, n_seq, t_len, seq_per_step, t_blk, chunk, sub):
    width = B_HEADS * B_DK
    assert seq_per_step == 1 or t_blk == t_len
    n_tb = t_len // t_blk
    rows = seq_per_step * t_blk
    kern = functools.partial(_hgrn_kernel, n_seq=seq_per_step, t_blk=t_blk, chunk=chunk, sub=sub)
    col0 = 3 * A_HEADS * HEAD_DIM // width
    blk = lambda off: pl.BlockSpec((rows, width), lambda s, t: (s * n_tb + t, col0 + off))
    vec = pl.BlockSpec((1, width), lambda s, t: (0, 0))
    state = pl.BlockSpec((seq_per_step, B_HEADS, B_DK, B_DK), lambda s, t: (s, 0, 0, 0))
    return pl.pallas_call(
        kern,
        grid=(n_seq // seq_per_step, n_tb),
        in_specs=[blk(0), blk(1), blk(2), blk(3), vec, vec, state],
        out_specs=[pl.BlockSpec((rows, width), lambda s, t: (s * n_tb + t, 0)), state],
        out_shape=[jax.ShapeDtypeStruct((n_seq * t_len, width), F32),
                   jax.ShapeDtypeStruct((n_seq, B_HEADS, B_DK, B_DK), F32)],
        scratch_shapes=[pltpu.VMEM((seq_per_step, B_HEADS // 2, LANES, LANES), F32)],
        compiler_params=_cparams("parallel", "arbitrary"),
        name="hgrn",
    )(proj, proj, proj, proj, lb.reshape(1, -1), jnp.tile(gnorm, B_HEADS).reshape(1, -1), s0)


def _diff_attn_prompt_kernel(q_ref, k_ref, v_ref, slope_ref, lam_ref, g_ref, o_ref,
                             q2_ref, m_ref, l_ref, acc_ref, *, tq, tk, post_scale):
    qi = pl.program_id(2)
    lo = lax.broadcasted_iota(jnp.int32, (tq, LANES), 1) < C_HD
    q = q_ref[...] * (C_HD ** -0.5 * LOG2E)
    q2_ref[...] = jnp.concatenate([jnp.where(lo, q, 0.0), jnp.where(lo, 0.0, q)], axis=0).astype(BF16)
    m_ref[...] = jnp.full(m_ref.shape, NEG, F32)
    l_ref[...] = jnp.zeros(l_ref.shape, F32)
    acc_ref[...] = jnp.zeros(acc_ref.shape, F32)
    slope = slope_ref[0:1, 0:1] * LOG2E
    base = (lax.broadcasted_iota(jnp.int32, (tq, tk), 0) - lax.broadcasted_iota(jnp.int32, (tq, tk), 1)).astype(F32)
    rel_bias = -slope * base

    def block(j, masked):
        rows = pl.ds(pl.multiple_of(j * tk, tk), tk)
        kb = k_ref[rows, :].astype(BF16)
        vb = v_ref[rows, :].astype(BF16)
        s = _dot_nt(q2_ref[...], kb)
        off = (qi * tq - j * tk).astype(F32)
        shift = -slope * off
        bias = jnp.where(base + off >= 0, rel_bias, NEG) if masked else rel_bias
        for mp in range(2):
            sm = s[mp * tq:(mp + 1) * tq] + bias
            m_prev = m_ref[mp]
            m_next = jnp.maximum(m_prev, jnp.max(sm, axis=1, keepdims=True) + shift)
            alpha = jnp.exp2(m_prev - m_next)
            p = jnp.exp2(sm - jnp.concatenate([m_next - shift] * (tk // LANES), axis=1))
            l_ref[mp] = alpha * l_ref[mp] + jnp.sum(p, axis=1, keepdims=True)
            acc_ref[mp] = alpha * acc_ref[mp] + _dot(p.astype(BF16), vb)
            m_ref[mp] = m_next

    n_full = (qi * tq) // tk

    def body(j, carry):
        block(j, False)
        return carry

    lax.fori_loop(0, n_full, body, 0)
    block(n_full, True)
    o = acc_ref[0] / l_ref[0] - lam_ref[0:1, 0:1] * (acc_ref[1] / l_ref[1])
    o_ref[...] = _rms(o, g_ref[...]) * post_scale


def diff_attn_prompt(proj, slopes, lam, subln, post_scale, batch, seq_len, *, tq, tk):
    assert tk % tq == 0 and seq_len % tk == 0
    nq = seq_len // tq
    kern = functools.partial(_diff_attn_prompt_kernel, tq=tq, tk=tk, post_scale=post_scale)
    kv = lambda off: pl.BlockSpec((seq_len, LANES), lambda b, h, i: (b, off + h))
    stat = pltpu.VMEM((2, tq, LANES), F32)
    return pl.pallas_call(
        kern,
        grid=(batch, C_HEADS, nq),
        in_specs=[pl.BlockSpec((tq, LANES), lambda b, h, i: (b * nq + i, h)),
                  kv(C_HEADS), kv(2 * C_HEADS),
                  pl.BlockSpec((None, 1, LANES), lambda b, h, i: (h, 0, 0)),
                  pl.BlockSpec((1, LANES), lambda b, h, i: (0, 0)),
                  pl.BlockSpec((1, LANES), lambda b, h, i: (0, 0))],
        out_specs=pl.BlockSpec((tq, LANES), lambda b, h, i: (b * nq + i, h)),
        out_shape=jax.ShapeDtypeStruct((batch * seq_len, C_HEADS * 2 * C_HD), F32),
        scratch_shapes=[pltpu.VMEM((2 * tq, LANES), BF16), stat, stat, stat],
        compiler_params=_cparams("parallel", "parallel", "arbitrary"),
        name="diff_attn_prompt",
    )(proj, proj, proj, slopes, lam, subln.reshape(1, -1))


def _diff_attn_sample_kernel(table_ref, q_ref, kn_ref, vn_ref, *rest, n_pages, t_new, post_scale):
    k_pages = rest[:n_pages]
    v_pages = rest[n_pages:2 * n_pages]
    bias_ref, lam_ref, g_ref, o_ref, kx_ref, vx_ref = rest[2 * n_pages:]
    del table_ref
    width = C_HEADS * 2 * C_HD
    n_past = n_pages * PAGE_SIZE
    n_ext = kx_ref.shape[0]
    for p in range(n_pages):
        for h in range(C_HEADS):
            cols = slice(h * 2 * C_HD, (h + 1) * 2 * C_HD)
            kx_ref[p * PAGE_SIZE:(p + 1) * PAGE_SIZE, cols] = k_pages[p][:, h, :].astype(BF16)
            vx_ref[p * PAGE_SIZE:(p + 1) * PAGE_SIZE, cols] = v_pages[p][:, h, :].astype(BF16)
    zpad = jnp.zeros((16 - t_new, width), F32)
    kx_ref[n_past:n_past + 16, :] = jnp.concatenate([kn_ref[...], zpad], axis=0).astype(BF16)
    vx_ref[n_past:n_past + 16, :] = jnp.concatenate([vn_ref[...], zpad], axis=0).astype(BF16)
    kx_ref[n_past + 16:, :] = jnp.zeros((n_ext - n_past - 16, width), BF16)
    vx_ref[n_past + 16:, :] = jnp.zeros((n_ext - n_past - 16, width), BF16)

    half = C_HEADS * t_new
    rmap = lax.broadcasted_iota(jnp.int32, (2 * half, width), 0) // t_new
    lmap = lax.broadcasted_iota(jnp.int32, (2 * half, width), 1) // C_HD
    own = ((rmap % C_HEADS) * 2 + rmap // C_HEADS) == lmap
    q = q_ref[...] * (C_HD ** -0.5)
    qs = jnp.where(own, jnp.concatenate([q] * (2 * C_HEADS), axis=0), 0.0).astype(BF16)
    s = _dot_nt(qs, kx_ref[...]) + bias_ref[...]
    m = jnp.max(s, axis=1, keepdims=True)
    p = jnp.exp(s - m)
    pn = p / jnp.sum(p, axis=1, keepdims=True)
    a = pn[:half] - lam_ref[0:1, 0:1] * pn[half:]
    o = _dot(a.astype(BF16), vx_ref[...])
    rhead = lax.broadcasted_iota(jnp.int32, (half, width), 0) // t_new
    lhead = lax.broadcasted_iota(jnp.int32, (half, width), 1) // (2 * C_HD)
    o = jnp.sum(jnp.where(rhead == lhead, o, 0.0).reshape(C_HEADS, t_new, width), axis=0)
    g = g_ref[...]
    outs = [_rms(o[:, h * LANES:(h + 1) * LANES], g) for h in range(C_HEADS)]
    o_ref[...] = jnp.concatenate(outs, axis=1) * post_scale


def diff_attn_sample(proj, cache_k, cache_v, table, lam, subln, post_scale, n_seq, t_new):
    width = C_HEADS * 2 * C_HD
    n_pages = table.shape[1]
    n_past = n_pages * PAGE_SIZE
    n_ext = n_past + LANES
    slopes = _alibi_slopes(C_HEADS)
    i = np.arange(t_new)[:, None]
    row = np.arange(n_ext)[None, :]
    dist = n_past + i - row
    ok = (dist >= 0) & (row < n_past + t_new)
    bias = np.stack([np.where(ok, -slopes[h] * dist, NEG) for h in range(C_HEADS)]).reshape(C_HEADS * t_new, n_ext)
    bias = jnp.asarray(np.concatenate([bias, bias], axis=0), F32)
    kern = functools.partial(_diff_attn_sample_kernel, n_pages=n_pages, t_new=t_new, post_scale=post_scale)
    new = lambda col: pl.BlockSpec((t_new, width), lambda s, tbl: (s, col))
    page = lambda p: pl.BlockSpec((None, PAGE_SIZE, C_HEADS, 2 * C_HD), lambda s, tbl, p=p: (tbl[s, p], 0, 0, 0))
    pages = [page(p) for p in range(n_pages)]
    return pl.pallas_call(
        kern,
        grid_spec=pltpu.PrefetchScalarGridSpec(
            num_scalar_prefetch=1,
            grid=(n_seq,),
            in_specs=[new(0), new(1), new(2)] + pages + pages + [
                pl.BlockSpec((2 * C_HEADS * t_new, n_ext), lambda s, tbl: (0, 0)),
                pl.BlockSpec((1, LANES), lambda s, tbl: (0, 0)),
                pl.BlockSpec((1, LANES), lambda s, tbl: (0, 0))],
            out_specs=pl.BlockSpec((t_new, width), lambda s, tbl: (s, 0)),
            scratch_shapes=[pltpu.VMEM((n_ext, width), BF16), pltpu.VMEM((n_ext, width), BF16)]),
        out_shape=jax.ShapeDtypeStruct((n_seq * t_new, width), F32),
        compiler_params=_cparams("arbitrary"),
        name="diff_attn_sample",
    )(table, proj, proj, proj, *([cache_k] * n_pages), *([cache_v] * n_pages), bias, lam, subln.reshape(1, -1))


S5_BLOCKS = 4
S5_BLOCK_STATES = (S5_GROUPS // S5_BLOCKS) * S5_STATE
S5_SEQ_ROWS = 8


def _gelu_tanh(x):
    return x * (0.5 * (1.0 + jnp.tanh(math.sqrt(2.0 / math.pi) * (x + 0.044715 * (x * x * x)))))


def _s5_kernel(u_ref, h0_ref, a_ref, bm_ref, cm_ref, d_ref, wglu_ref, bglu_ref, o_ref, hout_ref,
               utm_ref, hb_ref, otm_ref, hst_ref, *, n_seq, tb):
    t = pl.program_id(1)
    rows = S5_SEQ_ROWS * tb
    ns = S5_BLOCK_STATES

    @pl.when(t == 0)
    def _():
        hst_ref[...] = h0_ref[...]

    if n_seq < S5_SEQ_ROWS:
        utm_ref[...] = jnp.zeros(utm_ref.shape, F32)
    for k in range(S5_BLOCKS):
        for j in range(n_seq):
            utm_ref[k, pl.ds(j, tb, stride=S5_SEQ_ROWS), :] = u_ref[j, :, k * LANES:(k + 1) * LANES]
        hb_ref[k] = _dot(utm_ref[k].astype(BF16), bm_ref[k].astype(BF16))

    def step(i, hs):
        sel = pl.ds(pl.multiple_of(i * S5_SEQ_ROWS, S5_SEQ_ROWS), S5_SEQ_ROWS)
        new = []
        for k in range(S5_BLOCKS):
            h_re, h_im = hs[k]
            a_re = a_ref[k, 0:1, 0:ns]
            a_im = a_ref[k, 1:2, 0:ns]
            n_re = a_re * h_re - a_im * h_im + hb_ref[k, sel, 0:ns]
            n_im = a_re * h_im + a_im * h_re + hb_ref[k, sel, ns:2 * ns]
            hb_ref[k, sel, 0:ns] = n_re
            hb_ref[k, sel, ns:2 * ns] = n_im
            new.append((n_re, n_im))
        return tuple(new)

    hs = tuple((hst_ref[k, :, 0:ns], hst_ref[k, :, ns:2 * ns]) for k in range(S5_BLOCKS))
    hs = lax.fori_loop(0, tb, step, hs)
    for k in range(S5_BLOCKS):
        hst_ref[k, :, 0:ns] = hs[k][0]
        hst_ref[k, :, ns:2 * ns] = hs[k][1]

    ys = [_dot(hb_ref[k].astype(BF16), cm_ref[k].astype(BF16)) + d_ref[k] * utm_ref[k] for k in range(S5_BLOCKS)]
    z = _gelu_tanh(jnp.concatenate(ys, axis=1))
    gate = jax.nn.sigmoid(_dot(z.astype(BF16), wglu_ref[...].astype(BF16)) + bglu_ref[...])
    od = z * gate
    for k in range(S5_BLOCKS):
        otm_ref[k] = od[:, k * LANES:(k + 1) * LANES]
    for j in range(n_seq):
        for k in range(S5_BLOCKS):
            o_ref[j, :, k * LANES:(k + 1) * LANES] = otm_ref[k, pl.ds(j, tb, stride=S5_SEQ_ROWS), :]

    @pl.when(t == pl.num_programs(1) - 1)
    def _():
        hout_ref[...] = hst_ref[...]


def _s5_params(a_re, a_im, log_dt, b_re, b_im, c_re, c_im, d_skip):
    dt = jnp.exp(log_dt)[:, None]
    mag = jnp.exp(a_re * dt)
    ab_re, ab_im = mag * jnp.cos(a_im * dt), mag * jnp.sin(a_im * dt)
    den = a_re * a_re + a_im * a_im
    xr, xi = ab_re - 1.0, ab_im
    z_re = (xr * a_re + xi * a_im) / den
    z_im = (xi * a_re - xr * a_im) / den
    bb_re = z_re[..., None] * b_re - z_im[..., None] * b_im
    bb_im = z_re[..., None] * b_im + z_im[..., None] * b_re
    gb = S5_GROUPS // S5_BLOCKS
    eye = jnp.eye(gb, dtype=F32)

    def in_mat(bb):
        bb = bb.reshape(S5_BLOCKS, gb, S5_STATE, S5_GROUP_CH)
        return jnp.einsum('kgpc,gh->kgchp', bb, eye).reshape(S5_BLOCKS, gb * S5_GROUP_CH, gb * S5_STATE)

    def out_mat(cc):
        cc = cc.reshape(S5_BLOCKS, gb, S5_GROUP_CH, S5_STATE)
        return jnp.einsum('kgcp,gh->khpgc', cc, eye).reshape(S5_BLOCKS, gb * S5_STATE, gb * S5_GROUP_CH)

    bm = jnp.concatenate([in_mat(bb_re), in_mat(bb_im)], axis=2)
    cm = jnp.concatenate([out_mat(c_re), out_mat(-c_im)], axis=1)

    def lanes(x):
        x = x.reshape(S5_BLOCKS, gb * S5_STATE)
        return jnp.concatenate([x, x], axis=1)

    a = jnp.stack([lanes(ab_re), lanes(ab_im)], axis=1)
    dvec = d_skip.reshape(S5_BLOCKS, 1, gb * S5_GROUP_CH)
    return a, bm, cm, dvec


def _s5_state_to_blocks(h_re, h_im, n_rows):
    n = h_re.shape[0]
    gb = S5_GROUPS // S5_BLOCKS
    r = h_re.reshape(n, S5_BLOCKS, gb * S5_STATE)
    i = h_im.reshape(n, S5_BLOCKS, gb * S5_STATE)
    h = jnp.transpose(jnp.concatenate([r, i], axis=2), (1, 0, 2))
    if n_rows > n:
        h = jnp.concatenate([h, jnp.zeros((S5_BLOCKS, n_rows - n, h.shape[2]), F32)], axis=1)
    return h


def _s5_blocks_to_state(h, n):
    gb = S5_GROUPS // S5_BLOCKS
    h = jnp.transpose(h[:, :n], (1, 0, 2))
    re = h[:, :, :gb * S5_STATE].reshape(n, S5_GROUPS, S5_STATE)
    im = h[:, :, gb * S5_STATE:].reshape(n, S5_GROUPS, S5_STATE)
    return re, im


def s5(proj3, h0_blocks, params, wglu, bglu, *, seq_per_step, tb):
    n_seq, t_len, _ = proj3.shape
    a, bm, cm, dvec = params
    width = S5_GROUPS * S5_GROUP_CH
    ucol = proj3.shape[2] // width - 1
    n_groups = n_seq // seq_per_step
    rows = S5_SEQ_ROWS * tb
    kern = functools.partial(_s5_kernel, n_seq=seq_per_step, tb=tb)
    full = lambda shape: pl.BlockSpec(shape, lambda s, t: (0,) * len(shape))
    state = pl.BlockSpec((S5_BLOCKS, S5_SEQ_ROWS, 2 * S5_BLOCK_STATES), lambda s, t: (0, s, 0))
    return pl.pallas_call(
        kern,
        grid=(n_groups, t_len // tb),
        in_specs=[pl.BlockSpec((seq_per_step, tb, width), lambda s, t: (s, t, ucol)),
                  state, full(a.shape), full(bm.shape), full(cm.shape), full(dvec.shape),
                  full(wglu.shape), full((1, width))],
        out_specs=[pl.BlockSpec((seq_per_step, tb, width), lambda s, t: (s, t, 0)), state],
        out_shape=[jax.ShapeDtypeStruct((n_seq, t_len, width), F32),
                   jax.ShapeDtypeStruct(h0_blocks.shape, F32)],
        scratch_shapes=[pltpu.VMEM((S5_BLOCKS, rows, LANES), F32),
                        pltpu.VMEM((S5_BLOCKS, rows, 2 * S5_BLOCK_STATES), F32),
                        pltpu.VMEM((S5_BLOCKS, rows, LANES), F32),
                        pltpu.VMEM((S5_BLOCKS, S5_SEQ_ROWS, 2 * S5_BLOCK_STATES), F32)],
        compiler_params=_cparams("parallel", "arbitrary"),
        name="s5",
    )(proj3, h0_blocks, a, bm, cm, dvec, wglu, bglu.reshape(1, width))


def _trunk(x, p, past, *, n_seq, t_len):
    n = n_seq * t_len
    tm = min(n, 1024)
    prompt = past is None
    a_width = A_HEADS * HEAD_DIM
    c_width = C_HEADS * 2 * C_HD
    new = {}

    proj = rms_matmul(x, p['norm_mix'][0], p['w_in_even'][0], tm=tm, tn=512)
    ka = proj[:, a_width:2 * a_width].reshape(n_seq, t_len, A_HEADS, HEAD_DIM)
    va = proj[:, 2 * a_width:3 * a_width].reshape(n_seq, t_len, A_HEADS, HEAD_DIM)
    lb = jnp.cumsum(jax.nn.softmax(p['hgrn_lb'].astype(F32), axis=0), axis=0)[0]
    if prompt:
        keep = min(A_MAX_WINDOW, t_len)
        new['a_k'], new['a_v'] = ka[:, t_len - keep:], va[:, t_len - keep:]
        o_a = attn_a_prompt(proj, n_seq, t_len)
        s0 = jnp.zeros((n_seq, B_HEADS, B_DK, B_DK), F32)
        o_b, s_fin = hgrn(proj, lb, p['hgrn_gnorm'][0], s0, n_seq=n_seq, t_len=t_len,
                          seq_per_step=1, t_blk=512, chunk=64, sub=16)
    else:
        new['a_k'], new['a_v'] = ka, va
        o_a = attn_a_sample(proj, jnp.transpose(past['a_k'][0], (0, 2, 3, 1)),
                            jnp.transpose(past['a_v'][0], (0, 2, 3, 1)), n_seq, t_len)
        o_b, s_fin = hgrn(proj, lb, p['hgrn_gnorm'][0], past['hgrn'][0], n_seq=n_seq, t_len=t_len,
                          seq_per_step=16, t_blk=t_len, chunk=16, sub=16)
    new['hgrn'] = s_fin
    x = mix_out(x, o_a, o_b, p['w_out_even'][0], tm=tm)
    x = ffn(x, p['norm_ffn'][0], p['ffn_w_gate'][0], p['ffn_w_up'][0], p['ffn_w_down'][0], tm=tm, tf=256)

    proj = rms_matmul(x, p['norm_mix'][1], p['w_in_odd'][0], tm=tm, tn=512)
    new['c_k'] = proj[:, c_width:2 * c_width].reshape(n_seq, t_len, C_HEADS, 2 * C_HD)
    new['c_v'] = proj[:, 2 * c_width:3 * c_width].reshape(n_seq, t_len, C_HEADS, 2 * C_HD)
    lam_init = 0.8 - 0.6 * math.exp(-0.3 * 1)
    lam = (jnp.exp(jnp.sum(p['diff_lq1'][0] * p['diff_lk1'][0]))
           - jnp.exp(jnp.sum(p['diff_lq2'][0] * p['diff_lk2'][0])) + lam_init)
    lam = jnp.full((1, LANES), lam, F32)
    s5_params = _s5_params(p['s5_a_re'][0], p['s5_a_im'][0], p['s5_log_dt'][0], p['s5_b_re'][0], p['s5_b_im'][0],
                           p['s5_c_re'][0], p['s5_c_im'][0], p['s5_d'][0])
    proj3 = proj.reshape(n_seq, t_len, proj.shape[1])
    if prompt:
        slopes = jnp.asarray(np.broadcast_to(_alibi_slopes(C_HEADS)[:, None, None], (C_HEADS, 1, LANES)), F32)
        o_c = diff_attn_prompt(proj, slopes, lam, p['diff_subln'][0], 1.0 - lam_init, n_seq, t_len, tq=512, tk=1024)
        h0 = jnp.zeros((S5_BLOCKS, S5_SEQ_ROWS, 2 * S5_BLOCK_STATES), F32)
        o_d, h_fin = s5(proj3, h0, s5_params, p['s5_w_glu'][0], p['s5_b_glu'][0], seq_per_step=n_seq, tb=128)
    else:
        o_c = diff_attn_sample(proj, past['c_k'][0], past['c_v'][0],
                               past['page_table'], lam, p['diff_subln'][0], 1.0 - lam_init, n_seq, t_len)
        h0 = _s5_state_to_blocks(past['s5_re'][0], past['s5_im'][0], n_seq)
        o_d, h_fin = s5(proj3, h0, s5_params, p['s5_w_glu'][0], p['s5_b_glu'][0],
                        seq_per_step=S5_SEQ_ROWS, tb=t_len)
    new['s5_re'], new['s5_im'] = _s5_blocks_to_state(h_fin, n_seq)
    x = mix_out(x, o_c, o_d.reshape(n, -1), p['w_out_odd'][0], tm=tm)
    x = moe(x, p['norm_ffn'][1], p['moe_router_w'][0], p['moe_router_b'][0],
            p['moe_w_gate'][0], p['moe_w_up'][0], p['moe_w_down'][0], tm=tm, tf=256)
    y = rmsnorm(x, p['norm_final'], tm=tm)
    return y.reshape(n_seq, t_len, D_MODEL), {k: v[None] for k, v in new.items()}


def kernel(x_prompt, x_sample, cache_a_k, cache_a_v, state_hgrn, cache_c_k, cache_c_v, state_s5_re, state_s5_im,
           page_table, norm_mix, norm_ffn, norm_final, w_in_even, w_out_even, hgrn_lb, hgrn_gnorm,
           ffn_w_gate, ffn_w_up, ffn_w_down, w_in_odd, w_out_odd, diff_lq1, diff_lk1, diff_lq2, diff_lk2,
           diff_subln, s5_a_re, s5_a_im, s5_log_dt, s5_b_re, s5_b_im, s5_c_re, s5_c_im, s5_d, s5_w_glu,
           s5_b_glu, moe_router_w, moe_router_b, moe_w_gate, moe_w_up, moe_w_down):
    p = {
        'norm_mix': norm_mix, 'norm_ffn': norm_ffn, 'norm_final': norm_final,
        'w_in_even': w_in_even, 'w_out_even': w_out_even, 'hgrn_lb': hgrn_lb, 'hgrn_gnorm': hgrn_gnorm,
        'ffn_w_gate': ffn_w_gate, 'ffn_w_up': ffn_w_up, 'ffn_w_down': ffn_w_down,
        'w_in_odd': w_in_odd, 'w_out_odd': w_out_odd, 'diff_lq1': diff_lq1, 'diff_lk1': diff_lk1,
        'diff_lq2': diff_lq2, 'diff_lk2': diff_lk2, 'diff_subln': diff_subln,
        's5_a_re': s5_a_re, 's5_a_im': s5_a_im, 's5_log_dt': s5_log_dt, 's5_b_re': s5_b_re, 's5_b_im': s5_b_im,
        's5_c_re': s5_c_re, 's5_c_im': s5_c_im, 's5_d': s5_d, 's5_w_glu': s5_w_glu, 's5_b_glu': s5_b_glu,
        'moe_router_w': moe_router_w, 'moe_router_b': moe_router_b,
        'moe_w_gate': moe_w_gate, 'moe_w_up': moe_w_up, 'moe_w_down': moe_w_down,
    }
    past = {'a_k': cache_a_k, 'a_v': cache_a_v, 'hgrn': state_hgrn, 'c_k': cache_c_k, 'c_v': cache_c_v,
            's5_re': state_s5_re, 's5_im': state_s5_im, 'page_table': page_table}
    bp, tp, _ = x_prompt.shape
    bs, ts, _ = x_sample.shape
    y_prompt, sp = _trunk(x_prompt.reshape(bp * tp, D_MODEL), p, None, n_seq=bp, t_len=tp)
    y_sample, ss = _trunk(x_sample.reshape(bs * ts, D_MODEL), p, past, n_seq=bs, t_len=ts)
    names = ('a_k', 'a_v', 'hgrn', 'c_k', 'c_v', 's5_re', 's5_im')
    return (y_prompt, y_sample) + tuple(sp[k] for k in names) + tuple(ss[k] for k in names)
```

```python
import functools
import math

import jax
import jax.numpy as jnp
import numpy as np
from jax import lax
from jax.experimental import pallas as pl
from jax.experimental.pallas import tpu as pltpu

F32 = jnp.float32
BF16 = jnp.bfloat16

D_MODEL = 1024
HEAD_DIM = 64
A_HEADS = 8
A_GROUPS = ((128, 1), (512, 4), (2048, 16))
A_MAX_WINDOW = 2048
A_TILE = 128
B_HEADS = 8
B_DK = 64
C_HEADS = 4
C_HD = 64
S5_GROUPS = 32
S5_GROUP_CH = 16
S5_STATE = 64
N_EXPERTS = 8
PAGE_SIZE = 128
RMS_EPS = 1e-6
NEG = -1e30
LOG2E = math.log2(math.e)

LANES = 128
VMEM_LIMIT = 52 * 1024 * 1024


def _cparams(*sem):
    return pltpu.CompilerParams(dimension_semantics=sem, vmem_limit_bytes=VMEM_LIMIT)


def _dot(a, b):
    return jnp.dot(a, b, preferred_element_type=F32)


def _dot_nt(a, b):
    return lax.dot_general(a, b, (((1,), (1,)), ((), ())), preferred_element_type=F32)


def _dot_tn(a, b):
    return lax.dot_general(a, b, (((0,), (0,)), ((), ())), preferred_element_type=F32)


def _rms(x, g):
    return x * lax.rsqrt(jnp.mean(x * x, axis=-1, keepdims=True) + RMS_EPS) * g


def _alibi_slopes(n):
    return 2.0 ** (-8.0 * np.arange(1, n + 1) / n)


def _rms_matmul_kernel(x_ref, g_ref, w_ref, o_ref, xn_ref):
    @pl.when(pl.program_id(1) == 0)
    def _():
        xn_ref[...] = _rms(x_ref[...], g_ref[...]).astype(BF16)

    o_ref[...] = _dot(xn_ref[...], w_ref[...].astype(BF16))


def rms_matmul(x, g, w, *, tm, tn):
    n, d = x.shape
    nout = w.shape[1]
    return pl.pallas_call(
        _rms_matmul_kernel,
        grid=(n // tm, nout // tn),
        in_specs=[pl.BlockSpec((tm, d), lambda i, j: (i, 0)),
                  pl.BlockSpec((1, d), lambda i, j: (0, 0)),
                  pl.BlockSpec((d, tn), lambda i, j: (0, j))],
        out_specs=pl.BlockSpec((tm, tn), lambda i, j: (i, j)),
        out_shape=jax.ShapeDtypeStruct((n, nout), F32),
        scratch_shapes=[pltpu.VMEM((tm, d), BF16)],
        compiler_params=_cparams("parallel", "arbitrary"),
        name="rms_matmul",
    )(x, g.reshape(1, d), w)


def _mix_out_kernel(res_ref, a_ref, b_ref, wa_ref, wb_ref, o_ref):
    acc = _dot(a_ref[...].astype(BF16), wa_ref[...].astype(BF16))
    acc = acc + _dot(b_ref[...].astype(BF16), wb_ref[...].astype(BF16))
    o_ref[...] = res_ref[...] + acc


def mix_out(res, a, b, w, *, tm):
    n, d = res.shape
    ka = a.shape[1]
    return pl.pallas_call(
        _mix_out_kernel,
        grid=(n // tm,),
        in_specs=[pl.BlockSpec((tm, d), lambda i: (i, 0)),
                  pl.BlockSpec((tm, ka), lambda i: (i, 0)),
                  pl.BlockSpec((tm, ka), lambda i: (i, 0)),
                  pl.BlockSpec((ka, d), lambda i: (0, 0)),
                  pl.BlockSpec((ka, d), lambda i: (1, 0))],
        out_specs=pl.BlockSpec((tm, d), lambda i: (i, 0)),
        out_shape=jax.ShapeDtypeStruct((n, d), F32),
        compiler_params=_cparams("parallel"),
        name="mix_out",
    )(res, a, b, w, w)


def _ffn_kernel(x_ref, g_ref, wg_ref, wu_ref, wd_ref, o_ref, xn_ref, acc_ref):
    f = pl.program_id(1)

    @pl.when(f == 0)
    def _():
        xn_ref[...] = _rms(x_ref[...], g_ref[...]).astype(BF16)
        acc_ref[...] = jnp.zeros_like(acc_ref)

    xn = xn_ref[...]
    hg = _dot(xn, wg_ref[...].astype(BF16))
    hu = _dot(xn, wu_ref[...].astype(BF16))
    h = (jax.nn.silu(hg) * hu).astype(BF16)
    acc_ref[...] += _dot(h, wd_ref[...].astype(BF16))

    @pl.when(f == pl.num_programs(1) - 1)
    def _():
        o_ref[...] = x_ref[...] + acc_ref[...]


def ffn(x, g, wg, wu, wd, *, tm, tf):
    n, d = x.shape
    dff = wg.shape[1]
    return pl.pallas_call(
        _ffn_kernel,
        grid=(n // tm, dff // tf),
        in_specs=[pl.BlockSpec((tm, d), lambda i, f: (i, 0)),
                  pl.BlockSpec((1, d), lambda i, f: (0, 0)),
                  pl.BlockSpec((d, tf), lambda i, f: (0, f)),
                  pl.BlockSpec((d, tf), lambda i, f: (0, f)),
                  pl.BlockSpec((tf, d), lambda i, f: (f, 0))],
        out_specs=pl.BlockSpec((tm, d), lambda i, f: (i, 0)),
        out_shape=jax.ShapeDtypeStruct((n, d), F32),
        scratch_shapes=[pltpu.VMEM((tm, d), BF16), pltpu.VMEM((tm, d), F32)],
        compiler_params=_cparams("parallel", "arbitrary"),
        name="ffn",
    )(x, g.reshape(1, d), wg, wu, wd)


ROUTE_E1, ROUTE_E2, ROUTE_G1, ROUTE_G2, ROUTE_R1, ROUTE_R2 = range(6)


def _moe_route_kernel(x_ref, g_ref, rw_ref, rb_ref, hn_ref, route_ref, cnt_ref, run_ref):
    tm = x_ref.shape[0]
    lane = lax.broadcasted_iota(jnp.int32, (tm, LANES), 1)

    @pl.when(pl.program_id(0) == 0)
    def _():
        run_ref[...] = jnp.zeros_like(run_ref)

    hn = _rms(x_ref[...], g_ref[...])
    hn_ref[...] = hn
    logits = jnp.dot(hn, rw_ref[...], precision=lax.Precision.HIGHEST, preferred_element_type=F32) + rb_ref[...]
    lg = jnp.where(lane < N_EXPERTS, logits, NEG)
    m1 = jnp.max(lg, axis=1, keepdims=True)
    i1 = jnp.min(jnp.where(lg == m1, lane, LANES), axis=1, keepdims=True)
    lg2 = jnp.where(lane == i1, NEG, lg)
    m2 = jnp.max(lg2, axis=1, keepdims=True)
    i2 = jnp.min(jnp.where(lg2 == m2, lane, LANES), axis=1, keepdims=True)
    e2 = jnp.exp(m2 - m1)
    g1 = 1.0 / (1.0 + e2)
    g2 = e2 / (1.0 + e2)
    oh1 = lane == i1
    oh2 = lane == i2
    oh = jnp.where(oh1, 1.0, 0.0) + jnp.where(oh2, 1.0, 0.0)
    earlier = lax.broadcasted_iota(jnp.int32, (tm, tm), 1) < lax.broadcasted_iota(jnp.int32, (tm, tm), 0)
    before = _dot(jnp.where(earlier, 1.0, 0.0).astype(BF16), oh.astype(BF16)) + run_ref[...]
    r1 = jnp.sum(jnp.where(oh1, before, 0.0), axis=1, keepdims=True)
    r2 = jnp.sum(jnp.where(oh2, before, 0.0), axis=1, keepdims=True)
    rec = jnp.zeros((tm, LANES), F32)
    for slot, val in ((ROUTE_E1, i1.astype(F32)), (ROUTE_E2, i2.astype(F32)), (ROUTE_G1, g1), (ROUTE_G2, g2),
                      (ROUTE_R1, r1), (ROUTE_R2, r2)):
        rec = jnp.where(lane == slot, val, rec)
    route_ref[...] = rec
    run_ref[...] += jnp.sum(oh, axis=0, keepdims=True)
    cnt_ref[...] = run_ref[...]


def moe_route(x, g, rw, rb, *, tm):
    n, d = x.shape
    ne = rw.shape[1]
    rw_pad = jnp.zeros((d, LANES), F32).at[:, :ne].set(rw)
    rb_pad = jnp.zeros((1, LANES), F32).at[0, :ne].set(rb)
    return pl.pallas_call(
        _moe_route_kernel,
        grid=(n // tm,),
        in_specs=[pl.BlockSpec((tm, d), lambda i: (i, 0)),
                  pl.BlockSpec((1, d), lambda i: (0, 0)),
                  pl.BlockSpec((d, LANES), lambda i: (0, 0)),
                  pl.BlockSpec((1, LANES), lambda i: (0, 0))],
        out_specs=[pl.BlockSpec((tm, d), lambda i: (i, 0)),
                   pl.BlockSpec((tm, LANES), lambda i: (i, 0)),
                   pl.BlockSpec((1, LANES), lambda i: (0, 0))],
        out_shape=[jax.ShapeDtypeStruct((n, d), F32),
                   jax.ShapeDtypeStruct((n, LANES), F32),
                   jax.ShapeDtypeStruct((1, LANES), F32)],
        scratch_shapes=[pltpu.VMEM((1, LANES), F32)],
        compiler_params=_cparams("arbitrary"),
        name="moe_route",
    )(x, g.reshape(1, d), rw_pad, rb_pad)


MOE_DMA_CHUNK = 256


def _row_copy(src, src_row, dst, dst_row, sem):
    return pltpu.make_async_copy(src.at[pl.ds(src_row, 1)], dst.at[pl.ds(dst_row, 1)], sem)


def _moe_dispatch_kernel(pos_ref, hn_ref, xs_in_ref, xs_ref, sem):
    del xs_in_ref
    n0 = pl.program_id(0) * MOE_DMA_CHUNK

    def issue(r, carry):
        n = n0 + r
        _row_copy(hn_ref, n, xs_ref, pos_ref[2 * n], sem).start()
        _row_copy(hn_ref, n, xs_ref, pos_ref[2 * n + 1], sem).start()
        return carry

    lax.fori_loop(0, MOE_DMA_CHUNK, issue, 0)

    def drain(r, carry):
        _row_copy(hn_ref, 0, xs_ref, 0, sem).wait()
        _row_copy(hn_ref, 0, xs_ref, 0, sem).wait()
        return carry

    lax.fori_loop(0, MOE_DMA_CHUNK, drain, 0)


def moe_dispatch(pos_flat, hn, n_slots):
    n, d = hn.shape
    return pl.pallas_call(
        _moe_dispatch_kernel,
        grid_spec=pltpu.PrefetchScalarGridSpec(
            num_scalar_prefetch=1,
            grid=(n // MOE_DMA_CHUNK,),
            in_specs=[pl.BlockSpec(memory_space=pl.ANY), pl.BlockSpec(memory_space=pl.ANY)],
            out_specs=pl.BlockSpec(memory_space=pl.ANY),
            scratch_shapes=[pltpu.SemaphoreType.DMA(())]),
        out_shape=jax.ShapeDtypeStruct((n_slots, d), F32),
        input_output_aliases={2: 0},
        compiler_params=_cparams("arbitrary"),
        name="moe_dispatch",
    )(pos_flat, hn, jnp.zeros((n_slots, d), F32))


def _moe_experts_kernel(te_ref, nv_ref, xs_ref, wg_ref, wu_ref, wd_ref, ys_ref, xb_ref, acc_ref):
    t = pl.program_id(0)
    f = pl.program_id(1)
    del te_ref
    live = t < nv_ref[0]

    @pl.when(live & (f == 0))
    def _():
        xb_ref[...] = xs_ref[...].astype(BF16)
        acc_ref[...] = jnp.zeros_like(acc_ref)

    @pl.when(live)
    def _():
        xb = xb_ref[...]
        hg = _dot(xb, wg_ref[...].astype(BF16))
        hu = _dot(xb, wu_ref[...].astype(BF16))
        h = (jax.nn.silu(hg) * hu).astype(BF16)
        acc_ref[...] += _dot(h, wd_ref[...].astype(BF16))

    last = f == pl.num_programs(1) - 1

    @pl.when(live & last)
    def _():
        ys_ref[...] = acc_ref[...]

    @pl.when(jnp.logical_not(live) & last)
    def _():
        ys_ref[...] = jnp.zeros_like(ys_ref)


def moe_experts(tile_expert, n_live, xs, wg, wu, wd, *, tm, tf):
    n_slots, d = xs.shape
    dff = wg.shape[2]
    n_tiles = n_slots // tm
    fsel = lambda t, f, te, nv: jnp.where(t < nv[0], f, 0)
    tsel = lambda t, f, te, nv: jnp.minimum(t, nv[0] - 1)
    return pl.pallas_call(
        _moe_experts_kernel,
        grid_spec=pltpu.PrefetchScalarGridSpec(
            num_scalar_prefetch=2,
            grid=(n_tiles, dff // tf),
            in_specs=[pl.BlockSpec((tm, d), lambda t, f, te, nv: (tsel(t, f, te, nv), 0)),
                      pl.BlockSpec((None, d, tf), lambda t, f, te, nv: (te[t], 0, fsel(t, f, te, nv))),
                      pl.BlockSpec((None, d, tf), lambda t, f, te, nv: (te[t], 0, fsel(t, f, te, nv))),
                      pl.BlockSpec((None, tf, d), lambda t, f, te, nv: (te[t], fsel(t, f, te, nv), 0))],
            out_specs=pl.BlockSpec((tm, d), lambda t, f, te, nv: (t, 0)),
            scratch_shapes=[pltpu.VMEM((tm, d), BF16), pltpu.VMEM((tm, d), F32)]),
        out_shape=jax.ShapeDtypeStruct((n_slots, d), F32),
        compiler_params=_cparams("arbitrary", "arbitrary"),
        name="moe_experts",
    )(tile_expert, n_live, xs, wg, wu, wd)


def _moe_combine_kernel(pos_ref, x_ref, route_ref, ys_ref, g_ref, o_ref, rows_ref, sem):
    tm = x_ref.shape[0]
    n0 = pl.program_id(0) * tm

    def issue(r, carry):
        n = n0 + r
        _row_copy(ys_ref, pos_ref[2 * n], rows_ref.at[0], r, sem).start()
        _row_copy(ys_ref, pos_ref[2 * n + 1], rows_ref.at[1], r, sem).start()
        return carry

    lax.fori_loop(0, tm, issue, 0)

    def drain(r, carry):
        _row_copy(ys_ref, 0, rows_ref.at[0], 0, sem).wait()
        _row_copy(ys_ref, 0, rows_ref.at[1], 0, sem).wait()
        return carry

    lax.fori_loop(0, tm, drain, 0)
    route = route_ref[...]
    g1 = route[:, ROUTE_G1:ROUTE_G1 + 1]
    g2 = route[:, ROUTE_G2:ROUTE_G2 + 1]
    y = x_ref[...] + (g1 * rows_ref[0] + g2 * rows_ref[1])
    o_ref[...] = _rms(y, g_ref[...])


def moe_combine(pos_flat, x, route, ys, g_final, *, tm):
    n, d = x.shape
    return pl.pallas_call(
        _moe_combine_kernel,
        grid_spec=pltpu.PrefetchScalarGridSpec(
            num_scalar_prefetch=1,
            grid=(n // tm,),
            in_specs=[pl.BlockSpec((tm, d), lambda i, pos: (i, 0)),
                      pl.BlockSpec((tm, LANES), lambda i, pos: (i, 0)),
                      pl.BlockSpec(memory_space=pl.ANY),
                      pl.BlockSpec((1, d), lambda i, pos: (0, 0))],
            out_specs=pl.BlockSpec((tm, d), lambda i, pos: (i, 0)),
            scratch_shapes=[pltpu.VMEM((2, tm, d), F32), pltpu.SemaphoreType.DMA(())]),
        out_shape=jax.ShapeDtypeStruct((n, d), F32),
        compiler_params=_cparams("arbitrary"),
        name="moe_combine",
    )(pos_flat, x, route, ys, g_final.reshape(1, d))


def moe_block(x, g, rw, rb, wg, wu, wd, g_final, *, tm_route, tm_expert, tf, tm_combine):
    n, d = x.shape
    ne = rw.shape[1]
    hn, route, cnt = moe_route(x, g, rw, rb, tm=tm_route)
    cnt = cnt[0, :ne].astype(jnp.int32)
    padded = ((cnt + tm_expert - 1) // tm_expert) * tm_expert
    ends = jnp.cumsum(padded)
    starts = ends - padded
    experts = route[:, ROUTE_E1:ROUTE_E2 + 1].astype(jnp.int32)
    ranks = route[:, ROUTE_R1:ROUTE_R2 + 1].astype(jnp.int32)
    pos_flat = (starts[experts] + ranks).reshape(2 * n)
    n_slots = 2 * n + ne * tm_expert
    tile_start = jnp.arange(n_slots // tm_expert, dtype=jnp.int32) * tm_expert
    tile_expert = jnp.minimum(jnp.sum(tile_start[:, None] >= ends[None, :], axis=1), ne - 1).astype(jnp.int32)
    n_live = (ends[ne - 1:] // tm_expert).astype(jnp.int32)
    xs = moe_dispatch(pos_flat, hn, n_slots)
    ys = moe_experts(tile_expert, n_live, xs, wg, wu, wd, tm=tm_expert, tf=tf)
    return moe_combine(pos_flat, x, route, ys, g_final, tm=tm_combine)


def _attn_a_prompt_kernel(q_ref, k_ref, v_ref, bias_ref, o_ref,
                          kx_ref, vx_ref, m0_ref, l0_ref, m1_ref, l1_ref, acc_ref, *, seq_len):
    pad = A_MAX_WINDOW
    kx_ref[0:pad, :] = jnp.zeros((pad, LANES), F32)
    vx_ref[0:pad, :] = jnp.zeros((pad, LANES), F32)
    kx_ref[pad:pad + seq_len, :] = k_ref[...]
    vx_ref[pad:pad + seq_len, :] = v_ref[...]
    lo = lax.broadcasted_iota(jnp.int32, (A_TILE, LANES), 1) < HEAD_DIM
    n_tiles = seq_len // A_TILE

    for g, (_, d) in enumerate(A_GROUPS):
        shift = int(math.log2(d))

        def tile(i, carry, g=g, d=d, shift=shift):
            m = lax.shift_right_logical(i, shift)
            r = i & (d - 1)
            row0 = r + (d * A_TILE) * m
            if d == 1:
                row0 = pl.multiple_of(row0, A_TILE)
                qsel = pl.ds(row0, A_TILE)
                ksel = pl.ds(row0 + pad - A_TILE, 2 * A_TILE)
            else:
                qsel = pl.ds(row0, A_TILE, stride=d)
                ksel = pl.ds(row0 + pad - A_TILE * d, 2 * A_TILE, stride=d)
            q = q_ref[qsel, :] * (HEAD_DIM ** -0.5)
            q2 = jnp.concatenate([jnp.where(lo, q, 0.0), jnp.where(lo, 0.0, q)], axis=0).astype(BF16)
            kb = kx_ref[ksel, :].astype(BF16)
            vb = vx_ref[ksel, :].astype(BF16)
            s = _dot_nt(q2, kb)
            first = jnp.where(m == 0, 0, 1)
            s0 = s[:A_TILE] + bias_ref[g, 0, first]
            s1 = s[A_TILE:] + bias_ref[g, 1, first]
            mt0 = jnp.max(s0, axis=1, keepdims=True)
            mt1 = jnp.max(s1, axis=1, keepdims=True)
            if g == 0:
                mn0 = jnp.broadcast_to(mt0, (A_TILE, LANES))
                mn1 = jnp.broadcast_to(mt1, (A_TILE, LANES))
            else:
                mo0 = m0_ref[qsel, :]
                mo1 = m1_ref[qsel, :]
                mn0 = jnp.maximum(mo0, mt0)
                mn1 = jnp.maximum(mo1, mt1)
            p0 = jnp.exp(s0 - jnp.concatenate([mn0, mn0], axis=1))
            p1 = jnp.exp(s1 - jnp.concatenate([mn1, mn1], axis=1))
            ls0 = jnp.sum(p0, axis=1, keepdims=True)
            ls1 = jnp.sum(p1, axis=1, keepdims=True)
            pv0 = _dot(p0.astype(BF16), vb)
            pv1 = _dot(p1.astype(BF16), vb)
            pv = jnp.where(lo, pv0, pv1)
            if g == 0:
                l0_ref[qsel, :] = jnp.broadcast_to(ls0, (A_TILE, LANES))
                l1_ref[qsel, :] = jnp.broadcast_to(ls1, (A_TILE, LANES))
                acc_ref[qsel, :] = pv
            else:
                a0 = jnp.exp(mo0 - mn0)
                a1 = jnp.exp(mo1 - mn1)
                l0_ref[qsel, :] = a0 * l0_ref[qsel, :] + ls0
                l1_ref[qsel, :] = a1 * l1_ref[qsel, :] + ls1
                acc_ref[qsel, :] = jnp.where(lo, a0, a1) * acc_ref[qsel, :] + pv
            m0_ref[qsel, :] = mn0
            m1_ref[qsel, :] = mn1
            return carry

        lax.fori_loop(0, n_tiles, tile, 0, unroll=2)

    lo_full = lax.broadcasted_iota(jnp.int32, (seq_len, LANES), 1) < HEAD_DIM
    o_ref[...] = acc_ref[...] / jnp.where(lo_full, l0_ref[...], l1_ref[...])


def _attn_a_prompt_bias():
    slopes = _alibi_slopes(A_HEADS)
    i = np.arange(A_TILE)[:, None]
    c = np.arange(2 * A_TILE)[None, :]
    delta = A_TILE + i - c
    band = (delta >= 0) & (delta <= A_TILE)
    out = np.zeros((A_HEADS // 2, len(A_GROUPS), 2, 2, A_TILE, 2 * A_TILE), np.float32)
    for hp in range(A_HEADS // 2):
        for g, (_, d) in enumerate(A_GROUPS):
            for hs in range(2):
                b = -slopes[2 * hp + hs] * d * delta
                out[hp, g, hs, 1] = np.where(band, b, NEG)
                out[hp, g, hs, 0] = np.where(band & (c >= A_TILE), b, NEG)
    return jnp.asarray(out)


def attn_a_prompt(proj, batch, seq_len):
    nhp = A_HEADS // 2
    kern = functools.partial(_attn_a_prompt_kernel, seq_len=seq_len)
    blk = lambda off: pl.BlockSpec((seq_len, LANES), lambda b, hp: (b, off + hp))
    stat = pltpu.VMEM((seq_len, LANES), F32)
    return pl.pallas_call(
        kern,
        grid=(batch, nhp),
        in_specs=[blk(0), blk(nhp), blk(2 * nhp),
                  pl.BlockSpec((None, len(A_GROUPS), 2, 2, A_TILE, 2 * A_TILE), lambda b, hp: (hp, 0, 0, 0, 0, 0))],
        out_specs=pl.BlockSpec((seq_len, LANES), lambda b, hp: (b, hp)),
        out_shape=jax.ShapeDtypeStruct((batch * seq_len, A_HEADS * HEAD_DIM), F32),
        scratch_shapes=[pltpu.VMEM((A_MAX_WINDOW + seq_len, LANES), F32),
                        pltpu.VMEM((A_MAX_WINDOW + seq_len, LANES), F32),
                        stat, stat, stat, stat, stat],
        compiler_params=_cparams("parallel", "parallel"),
        name="attn_a_prompt",
    )(proj, proj, proj, _attn_a_prompt_bias())


def _attn_a_sample_kernel(q_ref, kn_ref, vn_ref, kt_ref, vt_ref, bias_ref, mult_ref, o_ref, *, t_new, n_cache):
    q = q_ref[...] * (HEAD_DIM ** -0.5)
    kn = kn_ref[...]
    vn = vn_ref[...]
    zpad = jnp.zeros((LANES - t_new, HEAD_DIM), F32)
    heads = [slice(h * HEAD_DIM, (h + 1) * HEAD_DIM) for h in range(A_HEADS)]
    s = []
    for h, cols in enumerate(heads):
        qh = q[:, cols].astype(BF16)
        s_cache = _dot(qh, kt_ref[h].astype(BF16))
        s_new = _dot_nt(qh, jnp.concatenate([kn[:, cols], zpad], axis=0).astype(BF16))
        s.append(jnp.concatenate([s_cache, s_new], axis=1))
    s = jnp.concatenate(s, axis=0) + bias_ref[...]
    m = jnp.max(s, axis=1, keepdims=True)
    p = mult_ref[...] * jnp.exp(s - m)
    p = p / jnp.sum(p, axis=1, keepdims=True)
    outs = []
    for h, cols in enumerate(heads):
        ph = p[h * t_new:(h + 1) * t_new].astype(BF16)
        o_cache = _dot_nt(ph[:, :n_cache], vt_ref[h].astype(BF16))
        o_new = _dot(ph[:, n_cache:], jnp.concatenate([vn[:, cols], zpad], axis=0).astype(BF16))
        outs.append(o_cache + o_new)
    o_ref[...] = jnp.concatenate(outs, axis=1)


def _attn_a_sample_consts(t_new, n_cache, n_ext):
    slopes = _alibi_slopes(A_HEADS)
    i = np.arange(t_new)[:, None]
    row = np.arange(n_ext)[None, :]
    delta = n_cache + i - row
    mult = np.zeros((t_new, n_ext), np.float32)
    for window, d in A_GROUPS:
        mult += ((delta >= 0) & (delta % d == 0) & (delta <= window) & (row < n_cache + t_new)).astype(np.float32)
    bias = np.stack([np.where(mult > 0, -slopes[h] * delta, NEG) for h in range(A_HEADS)])
    mult = np.broadcast_to(mult[None], (A_HEADS, t_new, n_ext))
    return (jnp.asarray(bias.reshape(A_HEADS * t_new, n_ext), F32),
            jnp.asarray(mult.reshape(A_HEADS * t_new, n_ext), F32))


def attn_a_sample(proj, cache_k, cache_v, n_seq, t_new):
    width = A_HEADS * HEAD_DIM
    n_cache = cache_k.shape[3]
    assert n_cache == A_MAX_WINDOW, "the dilated-attention sample kernel expects a full window buffer"
    n_ext = n_cache + LANES
    bias, mult = _attn_a_sample_consts(t_new, n_cache, n_ext)
    kern = functools.partial(_attn_a_sample_kernel, t_new=t_new, n_cache=n_cache)
    new = lambda col: pl.BlockSpec((t_new, width), lambda s: (s, col))
    cache = pl.BlockSpec((None, A_HEADS, HEAD_DIM, n_cache), lambda s: (s, 0, 0, 0))
    const = pl.BlockSpec((A_HEADS * t_new, n_ext), lambda s: (0, 0))
    return pl.pallas_call(
        kern,
        grid=(n_seq,),
        in_specs=[new(0), new(1), new(2), cache, cache, const, const],
        out_specs=pl.BlockSpec((t_new, width), lambda s: (s, 0)),
        out_shape=jax.ShapeDtypeStruct((n_seq * t_new, width), F32),
        compiler_params=_cparams("parallel"),
        name="attn_a_sample",
    )(proj, proj, proj, cache_k, cache_v, bias, mult)


def _hgrn_chunk(qraw, fraw, iv, graw, st, lb, gn, *, chunk, sub, valid):
    lo = lax.broadcasted_iota(jnp.int32, (chunk, LANES), 1) < B_DK
    rowi = lax.broadcasted_iota(jnp.int32, (chunk, LANES), 0)
    f = lb + (1.0 - lb) * jax.nn.sigmoid(fraw)
    logf = jnp.log(f)
    kk = 1.0 - f
    if valid < chunk:
        logf = jnp.where(rowi < valid, logf, 0.0)
        kk = jnp.where(rowi < valid, kk, 0.0)
    qh = jax.nn.silu(qraw) * (B_DK ** -0.5)
    tri_r = lax.broadcasted_iota(jnp.int32, (chunk, chunk), 0)
    tri_c = lax.broadcasted_iota(jnp.int32, (chunk, chunk), 1)
    causal = tri_c <= tri_r
    cum = jnp.dot(causal.astype(F32), logf, precision=lax.Precision.HIGHEST, preferred_element_type=F32)
    last = cum[chunk - 1:chunk, :]

    o = _dot_nt((qh * jnp.exp(cum)).astype(BF16), st.astype(BF16))

    att0, att1 = [], []
    lo_sub = lax.broadcasted_iota(jnp.int32, (sub, LANES), 1) < B_DK
    for blk in range(chunk // sub):
        r0 = blk * sub
        n = r0 + sub
        c0 = cum[r0 - 1:r0, :] if blk > 0 else jnp.zeros((1, LANES), F32)
        qt = qh[r0:n] * jnp.exp(cum[r0:n] - c0)
        q2 = jnp.concatenate([jnp.where(lo_sub, qt, 0.0), jnp.where(lo_sub, 0.0, qt)], axis=0).astype(BF16)
        kt = jnp.where(rowi < n, kk * jnp.exp(jnp.where(rowi < n, c0 - cum, 0.0)), 0.0).astype(BF16)
        a = _dot_nt(q2, kt)
        att0.append(a[:sub])
        att1.append(a[sub:])
    att0 = jnp.where(causal, jnp.concatenate(att0, axis=0), 0.0).astype(BF16)
    att1 = jnp.where(causal, jnp.concatenate(att1, axis=0), 0.0).astype(BF16)
    o = o + _dot(att0, jnp.where(lo, iv, 0.0).astype(BF16)) + _dot(att1, jnp.where(lo, 0.0, iv).astype(BF16))

    kend = (kk * jnp.exp(last - cum)).astype(BF16)
    upd = _dot_tn(iv.astype(BF16), kend)
    same = ((lax.broadcasted_iota(jnp.int32, (LANES, LANES), 0) < B_DK)
            == (lax.broadcasted_iota(jnp.int32, (LANES, LANES), 1) < B_DK))
    st_new = st * jnp.exp(last) + jnp.where(same, upd, 0.0)

    o2 = o * o
    ms0 = jnp.sum(jnp.where(lo, o2, 0.0), axis=1, keepdims=True)
    ms1 = jnp.sum(jnp.where(lo, 0.0, o2), axis=1, keepdims=True)
    ms = jnp.where(lo, ms0, ms1) * (1.0 / B_DK)
    out = o * lax.rsqrt(ms + RMS_EPS) * gn * jax.nn.silu(graw)
    return out, st_new


def _hgrn_kernel(q_ref, f_ref, i_ref, g_ref, lb_ref, gn_ref, s0_ref, o_ref, sout_ref, st_ref,
                 *, n_seq, t_blk, chunk, sub):
    tb = pl.program_id(1)
    nhp = B_HEADS // 2
    valid = min(t_blk, chunk)
    n_chunks = max(t_blk // chunk, 1)
    zero = jnp.zeros((B_DK, B_DK), F32)
    pairs = [slice(hp * LANES, (hp + 1) * LANES) for hp in range(nhp)]

    def one_seq(sq, carry):
        @pl.when(tb == 0)
        def _():
            for hp in range(nhp):
                s_bd = jnp.concatenate([jnp.concatenate([s0_ref[sq, 2 * hp], zero], axis=1),
                                        jnp.concatenate([zero, s0_ref[sq, 2 * hp + 1]], axis=1)], axis=0)
                st_ref[sq, hp] = s_bd.T

        def one_chunk(c, carry):
            base = pl.multiple_of(sq * t_blk + c * valid, 8)
            rows = pl.ds(base, valid)
            for hp, cols in enumerate(pairs):
                args = [r[rows, cols] for r in (q_ref, f_ref, i_ref, g_ref)]
                if valid < chunk:
                    args = [jnp.concatenate([a, jnp.zeros((chunk - valid, LANES), F32)], axis=0) for a in args]
                out, st = _hgrn_chunk(*args, st_ref[sq, hp], lb_ref[:, cols], gn_ref[:, cols],
                                      chunk=chunk, sub=sub, valid=valid)
                o_ref[rows, cols] = out[:valid]
                st_ref[sq, hp] = st
            return carry

        lax.fori_loop(0, n_chunks, one_chunk, 0)

        @pl.when(tb == pl.num_programs(1) - 1)
        def _():
            for hp in range(nhp):
                s_fin = st_ref[sq, hp].T
                sout_ref[sq, 2 * hp] = s_fin[:B_DK, :B_DK]
                sout_ref[sq, 2 * hp + 1] = s_fin[B_DK:, B_DK:]
        return carry

    lax.fori_loop(0, n_seq, one_seq, 0)


def hgrn(proj, lb, gnorm, s0, *, n_seq, t_len, seq_per_step, t_blk, chunk, sub):
    width = B_HEADS * B_DK
    assert seq_per_step == 1 or t_blk == t_len
    n_tb = t_len // t_blk
    rows = seq_per_step * t_blk
    kern = functools.partial(_hgrn_kernel, n_seq=seq_per_step, t_blk=t_blk, chunk=chunk, sub=sub)
    col0 = 3 * A_HEADS * HEAD_DIM // width
    blk = lambda off: pl.BlockSpec((rows, width), lambda s, t: (s * n_tb + t, col0 + off))
    vec = pl.BlockSpec((1, width), lambda s, t: (0, 0))
    state = pl.BlockSpec((seq_per_step, B_HEADS, B_DK, B_DK), lambda s, t: (s, 0, 0, 0))
    return pl.pallas_call(
        kern,
        grid=(n_seq // seq_per_step, n_tb),
        in_specs=[blk(0), blk(1), blk(2), blk(3), vec, vec, state],
        out_specs=[pl.BlockSpec((rows, width), lambda s, t: (s * n_tb + t, 0)), state],
        out_shape=[jax.ShapeDtypeStruct((n_seq * t_len, width), F32),
                   jax.ShapeDtypeStruct((n_seq, B_HEADS, B_DK, B_DK), F32)],
        scratch_shapes=[pltpu.VMEM((seq_per_step, B_HEADS // 2, LANES, LANES), F32)],
        compiler_params=_cparams("parallel", "arbitrary"),
        name="hgrn",
    )(proj, proj, proj, proj, lb.reshape(1, -1), jnp.tile(gnorm, B_HEADS).reshape(1, -1), s0)


def _diff_attn_prompt_kernel(q_ref, k_ref, v_ref, slope_ref, lam_ref, g_ref, o_ref,
                             q2_ref, m_ref, l_ref, acc_ref, *, tq, tk, post_scale):
    qi = pl.program_id(2)
    lo = lax.broadcasted_iota(jnp.int32, (tq, LANES), 1) < C_HD
    q = q_ref[...] * (C_HD ** -0.5 * LOG2E)
    q2_ref[...] = jnp.concatenate([jnp.where(lo, q, 0.0), jnp.where(lo, 0.0, q)], axis=0).astype(BF16)
    m_ref[...] = jnp.full(m_ref.shape, NEG, F32)
    l_ref[...] = jnp.zeros(l_ref.shape, F32)
    acc_ref[...] = jnp.zeros(acc_ref.shape, F32)
    slope = slope_ref[0:1, 0:1] * LOG2E
    base = (lax.broadcasted_iota(jnp.int32, (tq, tk), 0) - lax.broadcasted_iota(jnp.int32, (tq, tk), 1)).astype(F32)
    rel_bias = -slope * base

    def block(j, masked):
        rows = pl.ds(pl.multiple_of(j * tk, tk), tk)
        kb = k_ref[rows, :].astype(BF16)
        vb = v_ref[rows, :].astype(BF16)
        s = _dot_nt(q2_ref[...], kb)
        off = (qi * tq - j * tk).astype(F32)
        shift = -slope * off
        bias = jnp.where(base + off >= 0, rel_bias, NEG) if masked else rel_bias
        for mp in range(2):
            sm = s[mp * tq:(mp + 1) * tq] + bias
            m_prev = m_ref[mp]
            m_next = jnp.maximum(m_prev, jnp.max(sm, axis=1, keepdims=True) + shift)
            alpha = jnp.exp2(m_prev - m_next)
            p = jnp.exp2(sm - jnp.concatenate([m_next - shift] * (tk // LANES), axis=1))
            l_ref[mp] = alpha * l_ref[mp] + jnp.sum(p, axis=1, keepdims=True)
            acc_ref[mp] = alpha * acc_ref[mp] + _dot(p.astype(BF16), vb)
            m_ref[mp] = m_next

    n_full = (qi * tq) // tk

    def body(j, carry):
        block(j, False)
        return carry

    lax.fori_loop(0, n_full, body, 0)
    block(n_full, True)
    o = acc_ref[0] / l_ref[0] - lam_ref[0:1, 0:1] * (acc_ref[1] / l_ref[1])
    o_ref[...] = _rms(o, g_ref[...]) * post_scale


def diff_attn_prompt(proj, slopes, lam, subln, post_scale, batch, seq_len, *, tq, tk):
    assert tk % tq == 0 and seq_len % tk == 0
    nq = seq_len // tq
    kern = functools.partial(_diff_attn_prompt_kernel, tq=tq, tk=tk, post_scale=post_scale)
    kv = lambda off: pl.BlockSpec((seq_len, LANES), lambda b, h, i: (b, off + h))
    stat = pltpu.VMEM((2, tq, LANES), F32)
    return pl.pallas_call(
        kern,
        grid=(batch, C_HEADS, nq),
        in_specs=[pl.BlockSpec((tq, LANES), lambda b, h, i: (b * nq + i, h)),
                  kv(C_HEADS), kv(2 * C_HEADS),
                  pl.BlockSpec((None, 1, LANES), lambda b, h, i: (h, 0, 0)),
                  pl.BlockSpec((1, LANES), lambda b, h, i: (0, 0)),
                  pl.BlockSpec((1, LANES), lambda b, h, i: (0, 0))],
        out_specs=pl.BlockSpec((tq, LANES), lambda b, h, i: (b * nq + i, h)),
        out_shape=jax.ShapeDtypeStruct((batch * seq_len, C_HEADS * 2 * C_HD), F32),
        scratch_shapes=[pltpu.VMEM((2 * tq, LANES), BF16), stat, stat, stat],
        compiler_params=_cparams("parallel", "parallel", "arbitrary"),
        name="diff_attn_prompt",
    )(proj, proj, proj, slopes, lam, subln.reshape(1, -1))


def _diff_attn_sample_kernel(table_ref, q_ref, kn_ref, vn_ref, *rest, n_pages, t_new, post_scale):
    k_pages = rest[:n_pages]
    v_pages = rest[n_pages:2 * n_pages]
    bias_ref, lam_ref, g_ref, o_ref, kx_ref, vx_ref = rest[2 * n_pages:]
    del table_ref
    width = C_HEADS * 2 * C_HD
    n_past = n_pages * PAGE_SIZE
    n_ext = kx_ref.shape[0]
    for p in range(n_pages):
        for h in range(C_HEADS):
            cols = slice(h * 2 * C_HD, (h + 1) * 2 * C_HD)
            one_head = pl.ds(h, PAGE_SIZE, stride=C_HEADS)
            kx_ref[p * PAGE_SIZE:(p + 1) * PAGE_SIZE, cols] = k_pages[p][one_head, :].astype(BF16)
            vx_ref[p * PAGE_SIZE:(p + 1) * PAGE_SIZE, cols] = v_pages[p][one_head, :].astype(BF16)
    zpad = jnp.zeros((16 - t_new, width), F32)
    kx_ref[n_past:n_past + 16, :] = jnp.concatenate([kn_ref[...], zpad], axis=0).astype(BF16)
    vx_ref[n_past:n_past + 16, :] = jnp.concatenate([vn_ref[...], zpad], axis=0).astype(BF16)
    kx_ref[n_past + 16:, :] = jnp.zeros((n_ext - n_past - 16, width), BF16)
    vx_ref[n_past + 16:, :] = jnp.zeros((n_ext - n_past - 16, width), BF16)

    half = C_HEADS * t_new
    rmap = lax.broadcasted_iota(jnp.int32, (2 * half, width), 0) // t_new
    lmap = lax.broadcasted_iota(jnp.int32, (2 * half, width), 1) // C_HD
    own = ((rmap % C_HEADS) * 2 + rmap // C_HEADS) == lmap
    q = q_ref[...] * (C_HD ** -0.5)
    qs = jnp.where(own, jnp.concatenate([q] * (2 * C_HEADS), axis=0), 0.0).astype(BF16)
    s = _dot_nt(qs, kx_ref[...]) + bias_ref[...]
    m = jnp.max(s, axis=1, keepdims=True)
    p = jnp.exp(s - m)
    pn = p / jnp.sum(p, axis=1, keepdims=True)
    a = pn[:half] - lam_ref[0:1, 0:1] * pn[half:]
    o = _dot(a.astype(BF16), vx_ref[...])
    rhead = lax.broadcasted_iota(jnp.int32, (half, width), 0) // t_new
    lhead = lax.broadcasted_iota(jnp.int32, (half, width), 1) // (2 * C_HD)
    o = jnp.sum(jnp.where(rhead == lhead, o, 0.0).reshape(C_HEADS, t_new, width), axis=0)
    g = g_ref[...]
    outs = [_rms(o[:, h * LANES:(h + 1) * LANES], g) for h in range(C_HEADS)]
    o_ref[...] = jnp.concatenate(outs, axis=1) * post_scale


def diff_attn_sample(proj, cache_k, cache_v, table, lam, subln, post_scale, n_seq, t_new):
    width = C_HEADS * 2 * C_HD
    n_pages = table.shape[1]
    n_past = n_pages * PAGE_SIZE
    n_ext = n_past + LANES
    slopes = _alibi_slopes(C_HEADS)
    i = np.arange(t_new)[:, None]
    row = np.arange(n_ext)[None, :]
    dist = n_past + i - row
    ok = (dist >= 0) & (row < n_past + t_new)
    bias = np.stack([np.where(ok, -slopes[h] * dist, NEG) for h in range(C_HEADS)]).reshape(C_HEADS * t_new, n_ext)
    bias = jnp.asarray(np.concatenate([bias, bias], axis=0), F32)
    kern = functools.partial(_diff_attn_sample_kernel, n_pages=n_pages, t_new=t_new, post_scale=post_scale)
    new = lambda col: pl.BlockSpec((t_new, width), lambda s, tbl: (s, col))
    page = lambda p: pl.BlockSpec((None, PAGE_SIZE * C_HEADS, 2 * C_HD), lambda s, tbl, p=p: (tbl[s, p], 0, 0))
    pages = [page(p) for p in range(n_pages)]
    return pl.pallas_call(
        kern,
        grid_spec=pltpu.PrefetchScalarGridSpec(
            num_scalar_prefetch=1,
            grid=(n_seq,),
            in_specs=[new(0), new(1), new(2)] + pages + pages + [
                pl.BlockSpec((2 * C_HEADS * t_new, n_ext), lambda s, tbl: (0, 0)),
                pl.BlockSpec((1, LANES), lambda s, tbl: (0, 0)),
                pl.BlockSpec((1, LANES), lambda s, tbl: (0, 0))],
            out_specs=pl.BlockSpec((t_new, width), lambda s, tbl: (s, 0)),
            scratch_shapes=[pltpu.VMEM((n_ext, width), BF16), pltpu.VMEM((n_ext, width), BF16)]),
        out_shape=jax.ShapeDtypeStruct((n_seq * t_new, width), F32),
        compiler_params=_cparams("arbitrary"),
        name="diff_attn_sample",
    )(table, proj, proj, proj, *([cache_k] * n_pages), *([cache_v] * n_pages), bias, lam, subln.reshape(1, -1))


S5_BLOCKS = 4
S5_BLOCK_STATES = (S5_GROUPS // S5_BLOCKS) * S5_STATE
S5_SEQ_ROWS = 8


def _gelu_tanh(x):
    return x * (0.5 * (1.0 + jnp.tanh(math.sqrt(2.0 / math.pi) * (x + 0.044715 * (x * x * x)))))


def _s5_kernel(u_ref, h0_ref, a_ref, bm_ref, cm_ref, d_ref, wglu_ref, bglu_ref, o_ref, hout_ref,
               utm_ref, hb_ref, otm_ref, hst_ref, *, n_seq, tb):
    t = pl.program_id(1)
    rows = S5_SEQ_ROWS * tb
    ns = S5_BLOCK_STATES

    @pl.when(t == 0)
    def _():
        hst_ref[...] = h0_ref[...]

    if n_seq < S5_SEQ_ROWS:
        utm_ref[...] = jnp.zeros(utm_ref.shape, F32)
    for k in range(S5_BLOCKS):
        for j in range(n_seq):
            utm_ref[k, pl.ds(j, tb, stride=S5_SEQ_ROWS), :] = u_ref[j, :, k * LANES:(k + 1) * LANES]
        hb_ref[k] = _dot(utm_ref[k].astype(BF16), bm_ref[k].astype(BF16))

    def step(i, hs):
        sel = pl.ds(pl.multiple_of(i * S5_SEQ_ROWS, S5_SEQ_ROWS), S5_SEQ_ROWS)
        new = []
        for k in range(S5_BLOCKS):
            h_re, h_im = hs[k]
            a_re = a_ref[k, 0:1, 0:ns]
            a_im = a_ref[k, 1:2, 0:ns]
            n_re = a_re * h_re - a_im * h_im + hb_ref[k, sel, 0:ns]
            n_im = a_re * h_im + a_im * h_re + hb_ref[k, sel, ns:2 * ns]
            hb_ref[k, sel, 0:ns] = n_re
            hb_ref[k, sel, ns:2 * ns] = n_im
            new.append((n_re, n_im))
        return tuple(new)

    hs = tuple((hst_ref[k, :, 0:ns], hst_ref[k, :, ns:2 * ns]) for k in range(S5_BLOCKS))
    hs = lax.fori_loop(0, tb, step, hs)
    for k in range(S5_BLOCKS):
        hst_ref[k, :, 0:ns] = hs[k][0]
        hst_ref[k, :, ns:2 * ns] = hs[k][1]

    ys = [_dot(hb_ref[k].astype(BF16), cm_ref[k].astype(BF16)) + d_ref[k] * utm_ref[k] for k in range(S5_BLOCKS)]
    z = _gelu_tanh(jnp.concatenate(ys, axis=1))
    gate = jax.nn.sigmoid(_dot(z.astype(BF16), wglu_ref[...].astype(BF16)) + bglu_ref[...])
    od = z * gate
    for k in range(S5_BLOCKS):
        otm_ref[k] = od[:, k * LANES:(k + 1) * LANES]
    for j in range(n_seq):
        for k in range(S5_BLOCKS):
            o_ref[j, :, k * LANES:(k + 1) * LANES] = otm_ref[k, pl.ds(j, tb, stride=S5_SEQ_ROWS), :]

    @pl.when(t == pl.num_programs(1) - 1)
    def _():
        hout_ref[...] = hst_ref[...]


def _s5_params(a_re, a_im, log_dt, b_re, b_im, c_re, c_im, d_skip):
    dt = jnp.exp(log_dt)[:, None]
    mag = jnp.exp(a_re * dt)
    ab_re, ab_im = mag * jnp.cos(a_im * dt), mag * jnp.sin(a_im * dt)
    den = a_re * a_re + a_im * a_im
    xr, xi = ab_re - 1.0, ab_im
    z_re = (xr * a_re + xi * a_im) / den
    z_im = (xi * a_re - xr * a_im) / den
    bb_re = z_re[..., None] * b_re - z_im[..., None] * b_im
    bb_im = z_re[..., None] * b_im + z_im[..., None] * b_re
    gb = S5_GROUPS // S5_BLOCKS
    eye = jnp.eye(gb, dtype=F32)

    def in_mat(bb):
        bb = bb.reshape(S5_BLOCKS, gb, S5_STATE, S5_GROUP_CH)
        return jnp.einsum('kgpc,gh->kgchp', bb, eye).reshape(S5_BLOCKS, gb * S5_GROUP_CH, gb * S5_STATE)

    def out_mat(cc):
        cc = cc.reshape(S5_BLOCKS, gb, S5_GROUP_CH, S5_STATE)
        return jnp.einsum('kgcp,gh->khpgc', cc, eye).reshape(S5_BLOCKS, gb * S5_STATE, gb * S5_GROUP_CH)

    bm = jnp.concatenate([in_mat(bb_re), in_mat(bb_im)], axis=2)
    cm = jnp.concatenate([out_mat(c_re), out_mat(-c_im)], axis=1)

    def lanes(x):
        x = x.reshape(S5_BLOCKS, gb * S5_STATE)
        return jnp.concatenate([x, x], axis=1)

    a = jnp.stack([lanes(ab_re), lanes(ab_im)], axis=1)
    dvec = d_skip.reshape(S5_BLOCKS, 1, gb * S5_GROUP_CH)
    return a, bm, cm, dvec


def _s5_state_to_blocks(h_re, h_im, n_rows):
    n = h_re.shape[0]
    gb = S5_GROUPS // S5_BLOCKS
    r = h_re.reshape(n, S5_BLOCKS, gb * S5_STATE)
    i = h_im.reshape(n, S5_BLOCKS, gb * S5_STATE)
    h = jnp.transpose(jnp.concatenate([r, i], axis=2), (1, 0, 2))
    if n_rows > n:
        h = jnp.concatenate([h, jnp.zeros((S5_BLOCKS, n_rows - n, h.shape[2]), F32)], axis=1)
    return h


def _s5_blocks_to_state(h, n):
    gb = S5_GROUPS // S5_BLOCKS
    h = jnp.transpose(h[:, :n], (1, 0, 2))
    re = h[:, :, :gb * S5_STATE].reshape(n, S5_GROUPS, S5_STATE)
    im = h[:, :, gb * S5_STATE:].reshape(n, S5_GROUPS, S5_STATE)
    return re, im


def s5(proj3, h0_blocks, params, wglu, bglu, *, seq_per_step, tb):
    n_seq, t_len, _ = proj3.shape
    a, bm, cm, dvec = params
    width = S5_GROUPS * S5_GROUP_CH
    ucol = proj3.shape[2] // width - 1
    n_groups = n_seq // seq_per_step
    rows = S5_SEQ_ROWS * tb
    kern = functools.partial(_s5_kernel, n_seq=seq_per_step, tb=tb)
    full = lambda shape: pl.BlockSpec(shape, lambda s, t: (0,) * len(shape))
    state = pl.BlockSpec((S5_BLOCKS, S5_SEQ_ROWS, 2 * S5_BLOCK_STATES), lambda s, t: (0, s, 0))
    return pl.pallas_call(
        kern,
        grid=(n_groups, t_len // tb),
        in_specs=[pl.BlockSpec((seq_per_step, tb, width), lambda s, t: (s, t, ucol)),
                  state, full(a.shape), full(bm.shape), full(cm.shape), full(dvec.shape),
                  full(wglu.shape), full((1, width))],
        out_specs=[pl.BlockSpec((seq_per_step, tb, width), lambda s, t: (s, t, 0)), state],
        out_shape=[jax.ShapeDtypeStruct((n_seq, t_len, width), F32),
                   jax.ShapeDtypeStruct(h0_blocks.shape, F32)],
        scratch_shapes=[pltpu.VMEM((S5_BLOCKS, rows, LANES), F32),
                        pltpu.VMEM((S5_BLOCKS, rows, 2 * S5_BLOCK_STATES), F32),
                        pltpu.VMEM((S5_BLOCKS, rows, LANES), F32),
                        pltpu.VMEM((S5_BLOCKS, S5_SEQ_ROWS, 2 * S5_BLOCK_STATES), F32)],
        compiler_params=_cparams("parallel", "arbitrary"),
        name="s5",
    )(proj3, h0_blocks, a, bm, cm, dvec, wglu, bglu.reshape(1, width))


def _trunk(x, p, past, *, n_seq, t_len):
    n = n_seq * t_len
    tm = min(n, 1024)
    prompt = past is None
    a_width = A_HEADS * HEAD_DIM
    c_width = C_HEADS * 2 * C_HD
    new = {}

    proj = rms_matmul(x, p['norm_mix'][0], p['w_in_even'][0], tm=tm, tn=512)
    ka = proj[:, a_width:2 * a_width].reshape(n_seq, t_len, A_HEADS, HEAD_DIM)
    va = proj[:, 2 * a_width:3 * a_width].reshape(n_seq, t_len, A_HEADS, HEAD_DIM)
    lb = jnp.cumsum(jax.nn.softmax(p['hgrn_lb'].astype(F32), axis=0), axis=0)[0]
    if prompt:
        keep = min(A_MAX_WINDOW, t_len)
        new['a_k'], new['a_v'] = ka[:, t_len - keep:], va[:, t_len - keep:]
        o_a = attn_a_prompt(proj, n_seq, t_len)
        s0 = jnp.zeros((n_seq, B_HEADS, B_DK, B_DK), F32)
        o_b, s_fin = hgrn(proj, lb, p['hgrn_gnorm'][0], s0, n_seq=n_seq, t_len=t_len,
                          seq_per_step=1, t_blk=512, chunk=64, sub=16)
    else:
        new['a_k'], new['a_v'] = ka, va
        o_a = attn_a_sample(proj, jnp.transpose(past['a_k'][0], (0, 2, 3, 1)),
                            jnp.transpose(past['a_v'][0], (0, 2, 3, 1)), n_seq, t_len)
        o_b, s_fin = hgrn(proj, lb, p['hgrn_gnorm'][0], past['hgrn'][0], n_seq=n_seq, t_len=t_len,
                          seq_per_step=16, t_blk=t_len, chunk=16, sub=16)
    new['hgrn'] = s_fin
    x = mix_out(x, o_a, o_b, p['w_out_even'][0], tm=tm)
    x = ffn(x, p['norm_ffn'][0], p['ffn_w_gate'][0], p['ffn_w_up'][0], p['ffn_w_down'][0], tm=tm, tf=256)

    proj = rms_matmul(x, p['norm_mix'][1], p['w_in_odd'][0], tm=tm, tn=512)
    new['c_k'] = proj[:, c_width:2 * c_width].reshape(n_seq, t_len, C_HEADS, 2 * C_HD)
    new['c_v'] = proj[:, 2 * c_width:3 * c_width].reshape(n_seq, t_len, C_HEADS, 2 * C_HD)
    lam_init = 0.8 - 0.6 * math.exp(-0.3 * 1)
    lam = (jnp.exp(jnp.sum(p['diff_lq1'][0] * p['diff_lk1'][0]))
           - jnp.exp(jnp.sum(p['diff_lq2'][0] * p['diff_lk2'][0])) + lam_init)
    lam = jnp.full((1, LANES), lam, F32)
    s5_params = _s5_params(p['s5_a_re'][0], p['s5_a_im'][0], p['s5_log_dt'][0], p['s5_b_re'][0], p['s5_b_im'][0],
                           p['s5_c_re'][0], p['s5_c_im'][0], p['s5_d'][0])
    proj3 = proj.reshape(n_seq, t_len, proj.shape[1])
    if prompt:
        slopes = jnp.asarray(np.broadcast_to(_alibi_slopes(C_HEADS)[:, None, None], (C_HEADS, 1, LANES)), F32)
        o_c = diff_attn_prompt(proj, slopes, lam, p['diff_subln'][0], 1.0 - lam_init, n_seq, t_len, tq=512, tk=1024)
        h0 = jnp.zeros((S5_BLOCKS, S5_SEQ_ROWS, 2 * S5_BLOCK_STATES), F32)
        o_d, h_fin = s5(proj3, h0, s5_params, p['s5_w_glu'][0], p['s5_b_glu'][0], seq_per_step=n_seq, tb=128)
    else:
        n_phys = past['c_k'].shape[1]
        page_rows = (n_phys, PAGE_SIZE * C_HEADS, 2 * C_HD)
        o_c = diff_attn_sample(proj, past['c_k'][0].reshape(page_rows), past['c_v'][0].reshape(page_rows),
                               past['page_table'], lam, p['diff_subln'][0], 1.0 - lam_init, n_seq, t_len)
        h0 = _s5_state_to_blocks(past['s5_re'][0], past['s5_im'][0], n_seq)
        o_d, h_fin = s5(proj3, h0, s5_params, p['s5_w_glu'][0], p['s5_b_glu'][0],
                        seq_per_step=S5_SEQ_ROWS, tb=t_len)
    new['s5_re'], new['s5_im'] = _s5_blocks_to_state(h_fin, n_seq)
    x = mix_out(x, o_c, o_d.reshape(n, -1), p['w_out_odd'][0], tm=tm)
    y = moe_block(x, p['norm_ffn'][1], p['moe_router_w'][0], p['moe_router_b'][0],
                  p['moe_w_gate'][0], p['moe_w_up'][0], p['moe_w_down'][0], p['norm_final'],
                  tm_route=512, tm_expert=1024 if prompt else 256, tf=256, tm_combine=256)
    return y.reshape(n_seq, t_len, D_MODEL), {k: v[None] for k, v in new.items()}


def kernel(x_prompt, x_sample, cache_a_k, cache_a_v, state_hgrn, cache_c_k, cache_c_v, state_s5_re, state_s5_im,
           page_table, norm_mix, norm_ffn, norm_final, w_in_even, w_out_even, hgrn_lb, hgrn_gnorm,
           ffn_w_gate, ffn_w_up, ffn_w_down, w_in_odd, w_out_odd, diff_lq1, diff_lk1, diff_lq2, diff_lk2,
           diff_subln, s5_a_re, s5_a_im, s5_log_dt, s5_b_re, s5_b_im, s5_c_re, s5_c_im, s5_d, s5_w_glu,
           s5_b_glu, moe_router_w, moe_router_b, moe_w_gate, moe_w_up, moe_w_down):
    p = {
        'norm_mix': norm_mix, 'norm_ffn': norm_ffn, 'norm_final': norm_final,
        'w_in_even': w_in_even, 'w_out_even': w_out_even, 'hgrn_lb': hgrn_lb, 'hgrn_gnorm': hgrn_gnorm,
        'ffn_w_gate': ffn_w_gate, 'ffn_w_up': ffn_w_up, 'ffn_w_down': ffn_w_down,
        'w_in_odd': w_in_odd, 'w_out_odd': w_out_odd, 'diff_lq1': diff_lq1, 'diff_lk1': diff_lk1,
        'diff_lq2': diff_lq2, 'diff_lk2': diff_lk2, 'diff_subln': diff_subln,
        's5_a_re': s5_a_re, 's5_a_im': s5_a_im, 's5_log_dt': s5_log_dt, 's5_b_re': s5_b_re, 's5_b_im': s5_b_im,
        's5_c_re': s5_c_re, 's5_c_im': s5_c_im, 's5_d': s5_d, 's5_w_glu': s5_w_glu, 's5_b_glu': s5_b_glu,
        'moe_router_w': moe_router_w, 'moe_router_b': moe_router_b,
        'moe_w_gate': moe_w_gate, 'moe_w_up': moe_w_up, 'moe_w_down': moe_w_down,
    }
    past = {'a_k': cache_a_k, 'a_v': cache_a_v, 'hgrn': state_hgrn, 'c_k': cache_c_k, 'c_v': cache_c_v,
            's5_re': state_s5_re, 's5_im': state_s5_im, 'page_table': page_table}
    bp, tp, _ = x_prompt.shape
    bs, ts, _ = x_sample.shape
    y_prompt, sp = _trunk(x_prompt.reshape(bp * tp, D_MODEL), p, None, n_seq=bp, t_len=tp)
    y_sample, ss = _trunk(x_sample.reshape(bs * ts, D_MODEL), p, past, n_seq=bs, t_len=ts)
    names = ('a_k', 'a_v', 'hgrn', 'c_k', 'c_v', 's5_re', 's5_im')
    return (y_prompt, y_sample) + tuple(sp[k] for k in names) + tuple(ss[k] for k in names)
```

```python
import functools
import math

import jax
import jax.numpy as jnp
import numpy as np
from jax import lax
from jax.experimental import pallas as pl
from jax.experimental.pallas import tpu as pltpu

F32 = jnp.float32
BF16 = jnp.bfloat16

D_MODEL = 1024
HEAD_DIM = 64
A_HEADS = 8
A_GROUPS = ((128, 1), (512, 4), (2048, 16))
A_MAX_WINDOW = 2048
A_TILE = 128
B_HEADS = 8
B_DK = 64
C_HEADS = 4
C_HD = 64
S5_GROUPS = 32
S5_GROUP_CH = 16
S5_STATE = 64
N_EXPERTS = 8
PAGE_SIZE = 128
RMS_EPS = 1e-6
NEG = -1e30
LOG2E = math.log2(math.e)

LANES = 128
VMEM_LIMIT = 52 * 1024 * 1024


def _cparams(*sem):
    return pltpu.CompilerParams(dimension_semantics=sem, vmem_limit_bytes=VMEM_LIMIT)


def _dot(a, b):
    return jnp.dot(a, b, preferred_element_type=F32)


def _dot_nt(a, b):
    return lax.dot_general(a, b, (((1,), (1,)), ((), ())), preferred_element_type=F32)


def _dot_tn(a, b):
    return lax.dot_general(a, b, (((0,), (0,)), ((), ())), preferred_element_type=F32)


def _rms(x, g):
    return x * lax.rsqrt(jnp.mean(x * x, axis=-1, keepdims=True) + RMS_EPS) * g


def _alibi_slopes(n):
    return 2.0 ** (-8.0 * np.arange(1, n + 1) / n)


def _rms_matmul_kernel(x_ref, g_ref, w_ref, o_ref, xn_ref):
    @pl.when(pl.program_id(1) == 0)
    def _():
        xn_ref[...] = _rms(x_ref[...], g_ref[...]).astype(BF16)

    o_ref[...] = _dot(xn_ref[...], w_ref[...].astype(BF16))


def rms_matmul(x, g, w, *, tm, tn):
    n, d = x.shape
    nout = w.shape[1]
    return pl.pallas_call(
        _rms_matmul_kernel,
        grid=(n // tm, nout // tn),
        in_specs=[pl.BlockSpec((tm, d), lambda i, j: (i, 0)),
                  pl.BlockSpec((1, d), lambda i, j: (0, 0)),
                  pl.BlockSpec((d, tn), lambda i, j: (0, j))],
        out_specs=pl.BlockSpec((tm, tn), lambda i, j: (i, j)),
        out_shape=jax.ShapeDtypeStruct((n, nout), F32),
        scratch_shapes=[pltpu.VMEM((tm, d), BF16)],
        compiler_params=_cparams("parallel", "arbitrary"),
        name="rms_matmul",
    )(x, g.reshape(1, d), w)


def _mix_out_kernel(res_ref, a_ref, b_ref, wa_ref, wb_ref, o_ref):
    acc = _dot(a_ref[...].astype(BF16), wa_ref[...].astype(BF16))
    acc = acc + _dot(b_ref[...].astype(BF16), wb_ref[...].astype(BF16))
    o_ref[...] = res_ref[...] + acc


def mix_out(res, a, b, w, *, tm):
    n, d = res.shape
    ka = a.shape[1]
    return pl.pallas_call(
        _mix_out_kernel,
        grid=(n // tm,),
        in_specs=[pl.BlockSpec((tm, d), lambda i: (i, 0)),
                  pl.BlockSpec((tm, ka), lambda i: (i, 0)),
                  pl.BlockSpec((tm, ka), lambda i: (i, 0)),
                  pl.BlockSpec((ka, d), lambda i: (0, 0)),
                  pl.BlockSpec((ka, d), lambda i: (1, 0))],
        out_specs=pl.BlockSpec((tm, d), lambda i: (i, 0)),
        out_shape=jax.ShapeDtypeStruct((n, d), F32),
        compiler_params=_cparams("parallel"),
        name="mix_out",
    )(res, a, b, w, w)


def _ffn_kernel(x_ref, g_ref, wg_ref, wu_ref, wd_ref, o_ref, xn_ref, acc_ref):
    f = pl.program_id(1)

    @pl.when(f == 0)
    def _():
        xn_ref[...] = _rms(x_ref[...], g_ref[...]).astype(BF16)
        acc_ref[...] = jnp.zeros_like(acc_ref)

    xn = xn_ref[...]
    hg = _dot(xn, wg_ref[...].astype(BF16))
    hu = _dot(xn, wu_ref[...].astype(BF16))
    h = (jax.nn.silu(hg) * hu).astype(BF16)
    acc_ref[...] += _dot(h, wd_ref[...].astype(BF16))

    @pl.when(f == pl.num_programs(1) - 1)
    def _():
        o_ref[...] = x_ref[...] + acc_ref[...]


def ffn(x, g, wg, wu, wd, *, tm, tf):
    n, d = x.shape
    dff = wg.shape[1]
    return pl.pallas_call(
        _ffn_kernel,
        grid=(n // tm, dff // tf),
        in_specs=[pl.BlockSpec((tm, d), lambda i, f: (i, 0)),
                  pl.BlockSpec((1, d), lambda i, f: (0, 0)),
                  pl.BlockSpec((d, tf), lambda i, f: (0, f)),
                  pl.BlockSpec((d, tf), lambda i, f: (0, f)),
                  pl.BlockSpec((tf, d), lambda i, f: (f, 0))],
        out_specs=pl.BlockSpec((tm, d), lambda i, f: (i, 0)),
        out_shape=jax.ShapeDtypeStruct((n, d), F32),
        scratch_shapes=[pltpu.VMEM((tm, d), BF16), pltpu.VMEM((tm, d), F32)],
        compiler_params=_cparams("parallel", "arbitrary"),
        name="ffn",
    )(x, g.reshape(1, d), wg, wu, wd)


ROUTE_E1, ROUTE_E2, ROUTE_G1, ROUTE_G2, ROUTE_R1, ROUTE_R2 = range(6)


def _moe_route_kernel(x_ref, g_ref, rw_ref, rb_ref, cnt0_ref, hn_ref, route_ref, cnt_ref, run_ref):
    tm = x_ref.shape[0]
    lane = lax.broadcasted_iota(jnp.int32, (tm, LANES), 1)

    @pl.when(pl.program_id(0) == 0)
    def _():
        run_ref[...] = cnt0_ref[...]

    hn = _rms(x_ref[...], g_ref[...])
    hn_ref[...] = hn
    logits = jnp.dot(hn, rw_ref[...], precision=lax.Precision.HIGHEST, preferred_element_type=F32) + rb_ref[...]
    lg = jnp.where(lane < N_EXPERTS, logits, NEG)
    m1 = jnp.max(lg, axis=1, keepdims=True)
    i1 = jnp.min(jnp.where(lg == m1, lane, LANES), axis=1, keepdims=True)
    lg2 = jnp.where(lane == i1, NEG, lg)
    m2 = jnp.max(lg2, axis=1, keepdims=True)
    i2 = jnp.min(jnp.where(lg2 == m2, lane, LANES), axis=1, keepdims=True)
    e2 = jnp.exp(m2 - m1)
    g1 = 1.0 / (1.0 + e2)
    g2 = e2 / (1.0 + e2)
    oh1 = lane == i1
    oh2 = lane == i2
    oh = jnp.where(oh1, 1.0, 0.0) + jnp.where(oh2, 1.0, 0.0)
    earlier = lax.broadcasted_iota(jnp.int32, (tm, tm), 1) < lax.broadcasted_iota(jnp.int32, (tm, tm), 0)
    before = _dot(jnp.where(earlier, 1.0, 0.0).astype(BF16), oh.astype(BF16)) + run_ref[...]
    r1 = jnp.sum(jnp.where(oh1, before, 0.0), axis=1, keepdims=True)
    r2 = jnp.sum(jnp.where(oh2, before, 0.0), axis=1, keepdims=True)
    rec = jnp.zeros((tm, LANES), F32)
    for slot, val in ((ROUTE_E1, i1.astype(F32)), (ROUTE_E2, i2.astype(F32)), (ROUTE_G1, g1), (ROUTE_G2, g2),
                      (ROUTE_R1, r1), (ROUTE_R2, r2)):
        rec = jnp.where(lane == slot, val, rec)
    route_ref[...] = rec
    run_ref[...] += jnp.sum(oh, axis=0, keepdims=True)
    cnt_ref[...] = run_ref[...]


def moe_route(x, g, rw, rb, cnt0, *, tm):
    n, d = x.shape
    ne = rw.shape[1]
    rw_pad = jnp.zeros((d, LANES), F32).at[:, :ne].set(rw)
    rb_pad = jnp.zeros((1, LANES), F32).at[0, :ne].set(rb)
    return pl.pallas_call(
        _moe_route_kernel,
        grid=(n // tm,),
        in_specs=[pl.BlockSpec((tm, d), lambda i: (i, 0)),
                  pl.BlockSpec((1, d), lambda i: (0, 0)),
                  pl.BlockSpec((d, LANES), lambda i: (0, 0)),
                  pl.BlockSpec((1, LANES), lambda i: (0, 0)),
                  pl.BlockSpec((1, LANES), lambda i: (0, 0))],
        out_specs=[pl.BlockSpec((tm, d), lambda i: (i, 0)),
                   pl.BlockSpec((tm, LANES), lambda i: (i, 0)),
                   pl.BlockSpec((1, LANES), lambda i: (0, 0))],
        out_shape=[jax.ShapeDtypeStruct((n, d), F32),
                   jax.ShapeDtypeStruct((n, LANES), F32),
                   jax.ShapeDtypeStruct((1, LANES), F32)],
        scratch_shapes=[pltpu.VMEM((1, LANES), F32)],
        compiler_params=_cparams("arbitrary"),
        name="moe_route",
    )(x, g.reshape(1, d), rw_pad, rb_pad, cnt0)


MOE_DMA_CHUNK = 256


def _row_copy(src, src_row, dst, dst_row, sem):
    return pltpu.make_async_copy(src.at[pl.ds(src_row, 1)], dst.at[pl.ds(dst_row, 1)], sem)


def _moe_dispatch_kernel(pos_ref, hn_ref, xs_in_ref, xs_ref, sem):
    del xs_in_ref
    n0 = pl.program_id(0) * MOE_DMA_CHUNK

    def issue(r, carry):
        n = n0 + r
        _row_copy(hn_ref, r, xs_ref, pos_ref[2 * n], sem).start()
        _row_copy(hn_ref, r, xs_ref, pos_ref[2 * n + 1], sem).start()
        return carry

    lax.fori_loop(0, MOE_DMA_CHUNK, issue, 0)

    def drain(r, carry):
        _row_copy(hn_ref, 0, xs_ref, 0, sem).wait()
        _row_copy(hn_ref, 0, xs_ref, 0, sem).wait()
        return carry

    lax.fori_loop(0, MOE_DMA_CHUNK, drain, 0)


def moe_dispatch(pos_flat, hn, xs):
    n, d = hn.shape
    n_slots = xs.shape[0]
    return pl.pallas_call(
        _moe_dispatch_kernel,
        grid_spec=pltpu.PrefetchScalarGridSpec(
            num_scalar_prefetch=1,
            grid=(n // MOE_DMA_CHUNK,),
            in_specs=[pl.BlockSpec((MOE_DMA_CHUNK, d), lambda i, pos: (i, 0)), pl.BlockSpec(memory_space=pl.ANY)],
            out_specs=pl.BlockSpec(memory_space=pl.ANY),
            scratch_shapes=[pltpu.SemaphoreType.DMA(())]),
        out_shape=jax.ShapeDtypeStruct((n_slots, d), F32),
        input_output_aliases={2: 0},
        compiler_params=_cparams("arbitrary"),
        name="moe_dispatch",
    )(pos_flat, hn, xs)


def _moe_experts_kernel(te_ref, nv_ref, xs_ref, wg_ref, wu_ref, wd_ref, ys_ref, xb_ref, acc_ref):
    t = pl.program_id(0)
    f = pl.program_id(1)
    del te_ref
    live = t < nv_ref[0]

    @pl.when(live & (f == 0))
    def _():
        xb_ref[...] = xs_ref[...].astype(BF16)
        acc_ref[...] = jnp.zeros_like(acc_ref)

    @pl.when(live)
    def _():
        xb = xb_ref[...]
        hg = _dot(xb, wg_ref[...].astype(BF16))
        hu = _dot(xb, wu_ref[...].astype(BF16))
        h = (jax.nn.silu(hg) * hu).astype(BF16)
        acc_ref[...] += _dot(h, wd_ref[...].astype(BF16))

    last = f == pl.num_programs(1) - 1

    @pl.when(live & last)
    def _():
        ys_ref[...] = acc_ref[...]

    @pl.when(jnp.logical_not(live) & last)
    def _():
        ys_ref[...] = jnp.zeros_like(ys_ref)


def moe_experts(tile_expert, n_live, xs, wg, wu, wd, *, tm, tf):
    n_slots, d = xs.shape
    dff = wg.shape[2]
    n_tiles = n_slots // tm
    fsel = lambda t, f, te, nv: jnp.where(t < nv[0], f, 0)
    tsel = lambda t, f, te, nv: jnp.minimum(t, jnp.maximum(nv[0] - 1, 0))
    return pl.pallas_call(
        _moe_experts_kernel,
        grid_spec=pltpu.PrefetchScalarGridSpec(
            num_scalar_prefetch=2,
            grid=(n_tiles, dff // tf),
            in_specs=[pl.BlockSpec((tm, d), lambda t, f, te, nv: (tsel(t, f, te, nv), 0)),
                      pl.BlockSpec((None, d, tf), lambda t, f, te, nv: (te[t], 0, fsel(t, f, te, nv))),
                      pl.BlockSpec((None, d, tf), lambda t, f, te, nv: (te[t], 0, fsel(t, f, te, nv))),
                      pl.BlockSpec((None, tf, d), lambda t, f, te, nv: (te[t], fsel(t, f, te, nv), 0))],
            out_specs=pl.BlockSpec((tm, d), lambda t, f, te, nv: (t, 0)),
            scratch_shapes=[pltpu.VMEM((tm, d), BF16), pltpu.VMEM((tm, d), F32)]),
        out_shape=jax.ShapeDtypeStruct((n_slots, d), F32),
        compiler_params=_cparams("arbitrary", "arbitrary"),
        name="moe_experts",
    )(tile_expert, n_live, xs, wg, wu, wd)


def _moe_combine_kernel(pos_ref, x_ref, route_ref, ys_ref, g_ref, o_ref, rows_ref, sem):
    tm = x_ref.shape[0]
    n0 = pl.program_id(0) * tm

    def issue(r, carry):
        n = n0 + r
        _row_copy(ys_ref, pos_ref[2 * n], rows_ref.at[0], r, sem).start()
        _row_copy(ys_ref, pos_ref[2 * n + 1], rows_ref.at[1], r, sem).start()
        return carry

    lax.fori_loop(0, tm, issue, 0)

    def drain(r, carry):
        _row_copy(ys_ref, 0, rows_ref.at[0], 0, sem).wait()
        _row_copy(ys_ref, 0, rows_ref.at[1], 0, sem).wait()
        return carry

    lax.fori_loop(0, tm, drain, 0)
    route = route_ref[...]
    g1 = route[:, ROUTE_G1:ROUTE_G1 + 1]
    g2 = route[:, ROUTE_G2:ROUTE_G2 + 1]
    y = x_ref[...] + (g1 * rows_ref[0] + g2 * rows_ref[1])
    o_ref[...] = _rms(y, g_ref[...])


def moe_combine(pos_flat, x, route, ys, g_final, *, tm):
    n, d = x.shape
    return pl.pallas_call(
        _moe_combine_kernel,
        grid_spec=pltpu.PrefetchScalarGridSpec(
            num_scalar_prefetch=1,
            grid=(n // tm,),
            in_specs=[pl.BlockSpec((tm, d), lambda i, pos: (i, 0)),
                      pl.BlockSpec((tm, LANES), lambda i, pos: (i, 0)),
                      pl.BlockSpec(memory_space=pl.ANY),
                      pl.BlockSpec((1, d), lambda i, pos: (0, 0))],
            out_specs=pl.BlockSpec((tm, d), lambda i, pos: (i, 0)),
            scratch_shapes=[pltpu.VMEM((2, tm, d), F32), pltpu.SemaphoreType.DMA(())]),
        out_shape=jax.ShapeDtypeStruct((n, d), F32),
        compiler_params=_cparams("arbitrary"),
        name="moe_combine",
    )(pos_flat, x, route, ys, g_final.reshape(1, d))


def moe_block(xs_in, g, rw, rb, wg, wu, wd, g_final, *, tm_route, tm_expert, tf, tm_combine):
    d = xs_in[0].shape[1]
    ne = rw.shape[1]
    n_total = sum(x.shape[0] for x in xs_in)
    cnt = jnp.zeros((1, LANES), F32)
    routed = []
    for x in xs_in:
        hn, route, cnt = moe_route(x, g, rw, rb, cnt, tm=tm_route)
        routed.append((hn, route))
    cnt = cnt[0, :ne].astype(jnp.int32)
    padded = ((cnt + tm_expert - 1) // tm_expert) * tm_expert
    ends = jnp.cumsum(padded)
    starts = ends - padded
    n_slots = 2 * n_total + ne * tm_expert
    tile_start = jnp.arange(n_slots // tm_expert, dtype=jnp.int32) * tm_expert
    tile_expert = jnp.minimum(jnp.sum(tile_start[:, None] >= ends[None, :], axis=1), ne - 1).astype(jnp.int32)
    n_live = (ends[ne - 1:] // tm_expert).astype(jnp.int32)
    slots = jnp.zeros((n_slots, d), F32)
    pos = []
    for hn, route in routed:
        experts = route[:, ROUTE_E1:ROUTE_E2 + 1].astype(jnp.int32)
        ranks = route[:, ROUTE_R1:ROUTE_R2 + 1].astype(jnp.int32)
        pos.append((starts[experts] + ranks).reshape(-1))
        slots = moe_dispatch(pos[-1], hn, slots)
    ys = moe_experts(tile_expert, n_live, slots, wg, wu, wd, tm=tm_expert, tf=tf)
    return [moe_combine(pf, x, route, ys, g_final, tm=tm_combine) for pf, x, (_, route) in zip(pos, xs_in, routed)]


def _attn_a_prompt_kernel(q_ref, k_ref, v_ref, bias_ref, o_ref,
                          kx_ref, vx_ref, m0_ref, l0_ref, m1_ref, l1_ref, acc_ref, *, seq_len):
    pad = A_MAX_WINDOW
    kx_ref[0:pad, :] = jnp.zeros((pad, LANES), F32)
    vx_ref[0:pad, :] = jnp.zeros((pad, LANES), F32)
    kx_ref[pad:pad + seq_len, :] = k_ref[...]
    vx_ref[pad:pad + seq_len, :] = v_ref[...]
    lo = lax.broadcasted_iota(jnp.int32, (A_TILE, LANES), 1) < HEAD_DIM
    n_tiles = seq_len // A_TILE

    for g, (_, d) in enumerate(A_GROUPS):
        shift = int(math.log2(d))

        def tile(i, carry, g=g, d=d, shift=shift):
            m = lax.shift_right_logical(i, shift)
            r = i & (d - 1)
            row0 = r + (d * A_TILE) * m
            if d == 1:
                row0 = pl.multiple_of(row0, A_TILE)
                qsel = pl.ds(row0, A_TILE)
                ksel = pl.ds(row0 + pad - A_TILE, 2 * A_TILE)
            else:
                qsel = pl.ds(row0, A_TILE, stride=d)
                ksel = pl.ds(row0 + pad - A_TILE * d, 2 * A_TILE, stride=d)
            q = q_ref[qsel, :] * (HEAD_DIM ** -0.5)
            q2 = jnp.concatenate([jnp.where(lo, q, 0.0), jnp.where(lo, 0.0, q)], axis=0).astype(BF16)
            kb = kx_ref[ksel, :].astype(BF16)
            vb = vx_ref[ksel, :].astype(BF16)
            s = _dot_nt(q2, kb)
            first = jnp.where(m == 0, 0, 1)
            s0 = s[:A_TILE] + bias_ref[g, 0, first]
            s1 = s[A_TILE:] + bias_ref[g, 1, first]
            mt0 = jnp.max(s0, axis=1, keepdims=True)
            mt1 = jnp.max(s1, axis=1, keepdims=True)
            if g == 0:
                mn0 = jnp.broadcast_to(mt0, (A_TILE, LANES))
                mn1 = jnp.broadcast_to(mt1, (A_TILE, LANES))
            else:
                mo0 = m0_ref[qsel, :]
                mo1 = m1_ref[qsel, :]
                mn0 = jnp.maximum(mo0, mt0)
                mn1 = jnp.maximum(mo1, mt1)
            p0 = jnp.exp(s0 - jnp.concatenate([mn0, mn0], axis=1))
            p1 = jnp.exp(s1 - jnp.concatenate([mn1, mn1], axis=1))
            ls0 = jnp.sum(p0, axis=1, keepdims=True)
            ls1 = jnp.sum(p1, axis=1, keepdims=True)
            pv0 = _dot(p0.astype(BF16), vb)
            pv1 = _dot(p1.astype(BF16), vb)
            pv = jnp.where(lo, pv0, pv1)
            if g == 0:
                l0_ref[qsel, :] = jnp.broadcast_to(ls0, (A_TILE, LANES))
                l1_ref[qsel, :] = jnp.broadcast_to(ls1, (A_TILE, LANES))
                acc_ref[qsel, :] = pv
            else:
                a0 = jnp.exp(mo0 - mn0)
                a1 = jnp.exp(mo1 - mn1)
                l0_ref[qsel, :] = a0 * l0_ref[qsel, :] + ls0
                l1_ref[qsel, :] = a1 * l1_ref[qsel, :] + ls1
                acc_ref[qsel, :] = jnp.where(lo, a0, a1) * acc_ref[qsel, :] + pv
            m0_ref[qsel, :] = mn0
            m1_ref[qsel, :] = mn1
            return carry

        lax.fori_loop(0, n_tiles, tile, 0, unroll=2)

    lo_full = lax.broadcasted_iota(jnp.int32, (seq_len, LANES), 1) < HEAD_DIM
    o_ref[...] = acc_ref[...] / jnp.where(lo_full, l0_ref[...], l1_ref[...])


def _attn_a_prompt_bias():
    slopes = _alibi_slopes(A_HEADS)
    i = np.arange(A_TILE)[:, None]
    c = np.arange(2 * A_TILE)[None, :]
    delta = A_TILE + i - c
    band = (delta >= 0) & (delta <= A_TILE)
    out = np.zeros((A_HEADS // 2, len(A_GROUPS), 2, 2, A_TILE, 2 * A_TILE), np.float32)
    for hp in range(A_HEADS // 2):
        for g, (_, d) in enumerate(A_GROUPS):
            for hs in range(2):
                b = -slopes[2 * hp + hs] * d * delta
                out[hp, g, hs, 1] = np.where(band, b, NEG)
                out[hp, g, hs, 0] = np.where(band & (c >= A_TILE), b, NEG)
    return jnp.asarray(out)


def attn_a_prompt(proj, batch, seq_len):
    nhp = A_HEADS // 2
    kern = functools.partial(_attn_a_prompt_kernel, seq_len=seq_len)
    blk = lambda off: pl.BlockSpec((seq_len, LANES), lambda b, hp: (b, off + hp))
    stat = pltpu.VMEM((seq_len, LANES), F32)
    return pl.pallas_call(
        kern,
        grid=(batch, nhp),
        in_specs=[blk(0), blk(nhp), blk(2 * nhp),
                  pl.BlockSpec((None, len(A_GROUPS), 2, 2, A_TILE, 2 * A_TILE), lambda b, hp: (hp, 0, 0, 0, 0, 0))],
        out_specs=pl.BlockSpec((seq_len, LANES), lambda b, hp: (b, hp)),
        out_shape=jax.ShapeDtypeStruct((batch * seq_len, A_HEADS * HEAD_DIM), F32),
        scratch_shapes=[pltpu.VMEM((A_MAX_WINDOW + seq_len, LANES), F32),
                        pltpu.VMEM((A_MAX_WINDOW + seq_len, LANES), F32),
                        stat, stat, stat, stat, stat],
        compiler_params=_cparams("parallel", "parallel"),
        name="attn_a_prompt",
    )(proj, proj, proj, _attn_a_prompt_bias())


def _attn_a_sample_kernel(q_ref, kn_ref, vn_ref, kt_ref, vt_ref, bias_ref, mult_ref, o_ref, *, t_new, n_cache):
    q = q_ref[...] * (HEAD_DIM ** -0.5)
    kn = kn_ref[...]
    vn = vn_ref[...]
    zpad = jnp.zeros((LANES - t_new, HEAD_DIM), F32)
    heads = [slice(h * HEAD_DIM, (h + 1) * HEAD_DIM) for h in range(A_HEADS)]
    s = []
    for h, cols in enumerate(heads):
        qh = q[:, cols].astype(BF16)
        s_cache = _dot(qh, kt_ref[h].astype(BF16))
        s_new = _dot_nt(qh, jnp.concatenate([kn[:, cols], zpad], axis=0).astype(BF16))
        s.append(jnp.concatenate([s_cache, s_new], axis=1))
    s = jnp.concatenate(s, axis=0) + bias_ref[...]
    m = jnp.max(s, axis=1, keepdims=True)
    p = mult_ref[...] * jnp.exp(s - m)
    p = p / jnp.sum(p, axis=1, keepdims=True)
    outs = []
    for h, cols in enumerate(heads):
        ph = p[h * t_new:(h + 1) * t_new].astype(BF16)
        o_cache = _dot_nt(ph[:, :n_cache], vt_ref[h].astype(BF16))
        o_new = _dot(ph[:, n_cache:], jnp.concatenate([vn[:, cols], zpad], axis=0).astype(BF16))
        outs.append(o_cache + o_new)
    o_ref[...] = jnp.concatenate(outs, axis=1)


def _attn_a_sample_consts(t_new, n_cache, n_ext):
    slopes = _alibi_slopes(A_HEADS)
    i = np.arange(t_new)[:, None]
    row = np.arange(n_ext)[None, :]
    delta = n_cache + i - row
    mult = np.zeros((t_new, n_ext), np.float32)
    for window, d in A_GROUPS:
        mult += ((delta >= 0) & (delta % d == 0) & (delta <= window) & (row < n_cache + t_new)).astype(np.float32)
    bias = np.stack([np.where(mult > 0, -slopes[h] * delta, NEG) for h in range(A_HEADS)])
    mult = np.broadcast_to(mult[None], (A_HEADS, t_new, n_ext))
    return (jnp.asarray(bias.reshape(A_HEADS * t_new, n_ext), F32),
            jnp.asarray(mult.reshape(A_HEADS * t_new, n_ext), F32))


def attn_a_sample(proj, cache_k, cache_v, n_seq, t_new):
    width = A_HEADS * HEAD_DIM
    n_cache = cache_k.shape[3]
    assert n_cache == A_MAX_WINDOW, "the dilated-attention sample kernel expects a full window buffer"
    n_ext = n_cache + LANES
    bias, mult = _attn_a_sample_consts(t_new, n_cache, n_ext)
    kern = functools.partial(_attn_a_sample_kernel, t_new=t_new, n_cache=n_cache)
    new = lambda col: pl.BlockSpec((t_new, width), lambda s: (s, col))
    cache = pl.BlockSpec((None, A_HEADS, HEAD_DIM, n_cache), lambda s: (s, 0, 0, 0))
    const = pl.BlockSpec((A_HEADS * t_new, n_ext), lambda s: (0, 0))
    return pl.pallas_call(
        kern,
        grid=(n_seq,),
        in_specs=[new(0), new(1), new(2), cache, cache, const, const],
        out_specs=pl.BlockSpec((t_new, width), lambda s: (s, 0)),
        out_shape=jax.ShapeDtypeStruct((n_seq * t_new, width), F32),
        compiler_params=_cparams("parallel"),
        name="attn_a_sample",
    )(proj, proj, proj, cache_k, cache_v, bias, mult)


def _hgrn_chunk(qraw, fraw, iv, graw, st, lb, gn, *, chunk, sub, valid):
    lo = lax.broadcasted_iota(jnp.int32, (chunk, LANES), 1) < B_DK
    rowi = lax.broadcasted_iota(jnp.int32, (chunk, LANES), 0)
    f = lb + (1.0 - lb) * jax.nn.sigmoid(fraw)
    logf = jnp.log(f)
    kk = 1.0 - f
    if valid < chunk:
        logf = jnp.where(rowi < valid, logf, 0.0)
        kk = jnp.where(rowi < valid, kk, 0.0)
    qh = jax.nn.silu(qraw) * (B_DK ** -0.5)
    tri_r = lax.broadcasted_iota(jnp.int32, (chunk, chunk), 0)
    tri_c = lax.broadcasted_iota(jnp.int32, (chunk, chunk), 1)
    causal = tri_c <= tri_r
    cum = jnp.dot(causal.astype(F32), logf, precision=lax.Precision.HIGHEST, preferred_element_type=F32)
    last = cum[chunk - 1:chunk, :]

    o = _dot_nt((qh * jnp.exp(cum)).astype(BF16), st.astype(BF16))

    att0, att1 = [], []
    lo_sub = lax.broadcasted_iota(jnp.int32, (sub, LANES), 1) < B_DK
    for blk in range(chunk // sub):
        r0 = blk * sub
        n = r0 + sub
        c0 = cum[r0 - 1:r0, :] if blk > 0 else jnp.zeros((1, LANES), F32)
        qt = qh[r0:n] * jnp.exp(cum[r0:n] - c0)
        q2 = jnp.concatenate([jnp.where(lo_sub, qt, 0.0), jnp.where(lo_sub, 0.0, qt)], axis=0).astype(BF16)
        kt = jnp.where(rowi < n, kk * jnp.exp(jnp.where(rowi < n, c0 - cum, 0.0)), 0.0).astype(BF16)
        a = _dot_nt(q2, kt)
        att0.append(a[:sub])
        att1.append(a[sub:])
    att0 = jnp.where(causal, jnp.concatenate(att0, axis=0), 0.0).astype(BF16)
    att1 = jnp.where(causal, jnp.concatenate(att1, axis=0), 0.0).astype(BF16)
    o = o + _dot(att0, jnp.where(lo, iv, 0.0).astype(BF16)) + _dot(att1, jnp.where(lo, 0.0, iv).astype(BF16))

    kend = (kk * jnp.exp(last - cum)).astype(BF16)
    upd = _dot_tn(iv.astype(BF16), kend)
    same = ((lax.broadcasted_iota(jnp.int32, (LANES, LANES), 0) < B_DK)
            == (lax.broadcasted_iota(jnp.int32, (LANES, LANES), 1) < B_DK))
    st_new = st * jnp.exp(last) + jnp.where(same, upd, 0.0)

    o2 = o * o
    ms0 = jnp.sum(jnp.where(lo, o2, 0.0), axis=1, keepdims=True)
    ms1 = jnp.sum(jnp.where(lo, 0.0, o2), axis=1, keepdims=True)
    ms = jnp.where(lo, ms0, ms1) * (1.0 / B_DK)
    out = o * lax.rsqrt(ms + RMS_EPS) * gn * jax.nn.silu(graw)
    return out, st_new


def _hgrn_kernel(q_ref, f_ref, i_ref, g_ref, lb_ref, gn_ref, s0_ref, o_ref, sout_ref, st_ref,
                 *, n_seq, t_blk, chunk, sub):
    tb = pl.program_id(1)
    nhp = B_HEADS // 2
    valid = min(t_blk, chunk)
    n_chunks = max(t_blk // chunk, 1)
    zero = jnp.zeros((B_DK, B_DK), F32)
    pairs = [slice(hp * LANES, (hp + 1) * LANES) for hp in range(nhp)]

    def one_seq(sq, carry):
        @pl.when(tb == 0)
        def _():
            for hp in range(nhp):
                s_bd = jnp.concatenate([jnp.concatenate([s0_ref[sq, 2 * hp], zero], axis=1),
                                        jnp.concatenate([zero, s0_ref[sq, 2 * hp + 1]], axis=1)], axis=0)
                st_ref[sq, hp] = s_bd.T

        def one_chunk(c, carry):
            base = pl.multiple_of(sq * t_blk + c * valid, 8)
            rows = pl.ds(base, valid)
            for hp, cols in enumerate(pairs):
                args = [r[rows, cols] for r in (q_ref, f_ref, i_ref, g_ref)]
                if valid < chunk:
                    args = [jnp.concatenate([a, jnp.zeros((chunk - valid, LANES), F32)], axis=0) for a in args]
                out, st = _hgrn_chunk(*args, st_ref[sq, hp], lb_ref[:, cols], gn_ref[:, cols],
                                      chunk=chunk, sub=sub, valid=valid)
                o_ref[rows, cols] = out[:valid]
                st_ref[sq, hp] = st
            return carry

        lax.fori_loop(0, n_chunks, one_chunk, 0)

        @pl.when(tb == pl.num_programs(1) - 1)
        def _():
            for hp in range(nhp):
                s_fin = st_ref[sq, hp].T
                sout_ref[sq, 2 * hp] = s_fin[:B_DK, :B_DK]
                sout_ref[sq, 2 * hp + 1] = s_fin[B_DK:, B_DK:]
        return carry

    lax.fori_loop(0, n_seq, one_seq, 0)


def hgrn(proj, lb, gnorm, s0, *, n_seq, t_len, seq_per_step, t_blk, chunk, sub):
    width = B_HEADS * B_DK
    assert seq_per_step == 1 or t_blk == t_len
    n_tb = t_len // t_blk
    rows = seq_per_step * t_blk
    kern = functools.partial(_hgrn_kernel, n_seq=seq_per_step, t_blk=t_blk, chunk=chunk, sub=sub)
    col0 = 3 * A_HEADS * HEAD_DIM // width
    blk = lambda off: pl.BlockSpec((rows, width), lambda s, t: (s * n_tb + t, col0 + off))
    vec = pl.BlockSpec((1, width), lambda s, t: (0, 0))
    state = pl.BlockSpec((seq_per_step, B_HEADS, B_DK, B_DK), lambda s, t: (s, 0, 0, 0))
    return pl.pallas_call(
        kern,
        grid=(n_seq // seq_per_step, n_tb),
        in_specs=[blk(0), blk(1), blk(2), blk(3), vec, vec, state],
        out_specs=[pl.BlockSpec((rows, width), lambda s, t: (s * n_tb + t, 0)), state],
        out_shape=[jax.ShapeDtypeStruct((n_seq * t_len, width), F32),
                   jax.ShapeDtypeStruct((n_seq, B_HEADS, B_DK, B_DK), F32)],
        scratch_shapes=[pltpu.VMEM((seq_per_step, B_HEADS // 2, LANES, LANES), F32)],
        compiler_params=_cparams("parallel", "arbitrary"),
        name="hgrn",
    )(proj, proj, proj, proj, lb.reshape(1, -1), jnp.tile(gnorm, B_HEADS).reshape(1, -1), s0)


def _diff_attn_prompt_kernel(q_ref, k_ref, v_ref, slope_ref, lam_ref, g_ref, o_ref,
                             q2_ref, m_ref, l_ref, acc_ref, *, tq, tk, post_scale):
    qi = pl.program_id(2)
    lo = lax.broadcasted_iota(jnp.int32, (tq, LANES), 1) < C_HD
    q = q_ref[...] * (C_HD ** -0.5 * LOG2E)
    q2_ref[...] = jnp.concatenate([jnp.where(lo, q, 0.0), jnp.where(lo, 0.0, q)], axis=0).astype(BF16)
    m_ref[...] = jnp.full(m_ref.shape, NEG, F32)
    l_ref[...] = jnp.zeros(l_ref.shape, F32)
    acc_ref[...] = jnp.zeros(acc_ref.shape, F32)
    slope = slope_ref[0:1, 0:1] * LOG2E
    base = (lax.broadcasted_iota(jnp.int32, (tq, tk), 0) - lax.broadcasted_iota(jnp.int32, (tq, tk), 1)).astype(F32)
    rel_bias = -slope * base

    def block(j, masked):
        rows = pl.ds(pl.multiple_of(j * tk, tk), tk)
        kb = k_ref[rows, :].astype(BF16)
        vb = v_ref[rows, :].astype(BF16)
        s = _dot_nt(q2_ref[...], kb)
        off = (qi * tq - j * tk).astype(F32)
        shift = -slope * off
        bias = jnp.where(base + off >= 0, rel_bias, NEG) if masked else rel_bias
        for mp in range(2):
            sm = s[mp * tq:(mp + 1) * tq] + bias
            m_prev = m_ref[mp]
            m_next = jnp.maximum(m_prev, jnp.max(sm, axis=1, keepdims=True) + shift)
            alpha = jnp.exp2(m_prev - m_next)
            p = jnp.exp2(sm - jnp.concatenate([m_next - shift] * (tk // LANES), axis=1))
            l_ref[mp] = alpha * l_ref[mp] + jnp.sum(p, axis=1, keepdims=True)
            acc_ref[mp] = alpha * acc_ref[mp] + _dot(p.astype(BF16), vb)
            m_ref[mp] = m_next

    n_full = (qi * tq) // tk

    def body(j, carry):
        block(j, False)
        return carry

    lax.fori_loop(0, n_full, body, 0)
    block(n_full, True)
    o = acc_ref[0] / l_ref[0] - lam_ref[0:1, 0:1] * (acc_ref[1] / l_ref[1])
    o_ref[...] = _rms(o, g_ref[...]) * post_scale


def diff_attn_prompt(proj, slopes, lam, subln, post_scale, batch, seq_len, *, tq, tk):
    assert tk % tq == 0 and seq_len % tk == 0
    nq = seq_len // tq
    kern = functools.partial(_diff_attn_prompt_kernel, tq=tq, tk=tk, post_scale=post_scale)
    kv = lambda off: pl.BlockSpec((seq_len, LANES), lambda b, h, i: (b, off + h))
    stat = pltpu.VMEM((2, tq, LANES), F32)
    return pl.pallas_call(
        kern,
        grid=(batch, C_HEADS, nq),
        in_specs=[pl.BlockSpec((tq, LANES), lambda b, h, i: (b * nq + i, h)),
                  kv(C_HEADS), kv(2 * C_HEADS),
                  pl.BlockSpec((None, 1, LANES), lambda b, h, i: (h, 0, 0)),
                  pl.BlockSpec((1, LANES), lambda b, h, i: (0, 0)),
                  pl.BlockSpec((1, LANES), lambda b, h, i: (0, 0))],
        out_specs=pl.BlockSpec((tq, LANES), lambda b, h, i: (b * nq + i, h)),
        out_shape=jax.ShapeDtypeStruct((batch * seq_len, C_HEADS * 2 * C_HD), F32),
        scratch_shapes=[pltpu.VMEM((2 * tq, LANES), BF16), stat, stat, stat],
        compiler_params=_cparams("parallel", "parallel", "arbitrary"),
        name="diff_attn_prompt",
    )(proj, proj, proj, slopes, lam, subln.reshape(1, -1))


def _diff_attn_sample_kernel(table_ref, q_ref, kn_ref, vn_ref, *rest, n_pages, t_new, post_scale):
    k_pages = rest[:n_pages]
    v_pages = rest[n_pages:2 * n_pages]
    bias_ref, lam_ref, g_ref, o_ref, kx_ref, vx_ref = rest[2 * n_pages:]
    del table_ref
    width = C_HEADS * 2 * C_HD
    n_past = n_pages * PAGE_SIZE
    n_ext = kx_ref.shape[0]
    for p in range(n_pages):
        for h in range(C_HEADS):
            cols = slice(h * 2 * C_HD, (h + 1) * 2 * C_HD)
            one_head = pl.ds(h, PAGE_SIZE, stride=C_HEADS)
            kx_ref[p * PAGE_SIZE:(p + 1) * PAGE_SIZE, cols] = k_pages[p][one_head, :].astype(BF16)
            vx_ref[p * PAGE_SIZE:(p + 1) * PAGE_SIZE, cols] = v_pages[p][one_head, :].astype(BF16)
    zpad = jnp.zeros((16 - t_new, width), F32)
    kx_ref[n_past:n_past + 16, :] = jnp.concatenate([kn_ref[...], zpad], axis=0).astype(BF16)
    vx_ref[n_past:n_past + 16, :] = jnp.concatenate([vn_ref[...], zpad], axis=0).astype(BF16)
    kx_ref[n_past + 16:, :] = jnp.zeros((n_ext - n_past - 16, width), BF16)
    vx_ref[n_past + 16:, :] = jnp.zeros((n_ext - n_past - 16, width), BF16)

    half = C_HEADS * t_new
    rmap = lax.broadcasted_iota(jnp.int32, (2 * half, width), 0) // t_new
    lmap = lax.broadcasted_iota(jnp.int32, (2 * half, width), 1) // C_HD
    own = ((rmap % C_HEADS) * 2 + rmap // C_HEADS) == lmap
    q = q_ref[...] * (C_HD ** -0.5)
    qs = jnp.where(own, jnp.concatenate([q] * (2 * C_HEADS), axis=0), 0.0).astype(BF16)
    s = _dot_nt(qs, kx_ref[...]) + bias_ref[...]
    m = jnp.max(s, axis=1, keepdims=True)
    p = jnp.exp(s - m)
    pn = p / jnp.sum(p, axis=1, keepdims=True)
    a = pn[:half] - lam_ref[0:1, 0:1] * pn[half:]
    o = _dot(a.astype(BF16), vx_ref[...])
    rhead = lax.broadcasted_iota(jnp.int32, (half, width), 0) // t_new
    lhead = lax.broadcasted_iota(jnp.int32, (half, width), 1) // (2 * C_HD)
    o = jnp.sum(jnp.where(rhead == lhead, o, 0.0).reshape(C_HEADS, t_new, width), axis=0)
    g = g_ref[...]
    outs = [_rms(o[:, h * LANES:(h + 1) * LANES], g) for h in range(C_HEADS)]
    o_ref[...] = jnp.concatenate(outs, axis=1) * post_scale


def diff_attn_sample(proj, cache_k, cache_v, table, lam, subln, post_scale, n_seq, t_new):
    width = C_HEADS * 2 * C_HD
    n_pages = table.shape[1]
    n_past = n_pages * PAGE_SIZE
    n_ext = n_past + LANES
    slopes = _alibi_slopes(C_HEADS)
    i = np.arange(t_new)[:, None]
    row = np.arange(n_ext)[None, :]
    dist = n_past + i - row
    ok = (dist >= 0) & (row < n_past + t_new)
    bias = np.stack([np.where(ok, -slopes[h] * dist, NEG) for h in range(C_HEADS)]).reshape(C_HEADS * t_new, n_ext)
    bias = jnp.asarray(np.concatenate([bias, bias], axis=0), F32)
    kern = functools.partial(_diff_attn_sample_kernel, n_pages=n_pages, t_new=t_new, post_scale=post_scale)
    new = lambda col: pl.BlockSpec((t_new, width), lambda s, tbl: (s, col))
    page = lambda p: pl.BlockSpec((None, PAGE_SIZE * C_HEADS, 2 * C_HD), lambda s, tbl, p=p: (tbl[s, p], 0, 0))
    pages = [page(p) for p in range(n_pages)]
    return pl.pallas_call(
        kern,
        grid_spec=pltpu.PrefetchScalarGridSpec(
            num_scalar_prefetch=1,
            grid=(n_seq,),
            in_specs=[new(0), new(1), new(2)] + pages + pages + [
                pl.BlockSpec((2 * C_HEADS * t_new, n_ext), lambda s, tbl: (0, 0)),
                pl.BlockSpec((1, LANES), lambda s, tbl: (0, 0)),
                pl.BlockSpec((1, LANES), lambda s, tbl: (0, 0))],
            out_specs=pl.BlockSpec((t_new, width), lambda s, tbl: (s, 0)),
            scratch_shapes=[pltpu.VMEM((n_ext, width), BF16), pltpu.VMEM((n_ext, width), BF16)]),
        out_shape=jax.ShapeDtypeStruct((n_seq * t_new, width), F32),
        compiler_params=_cparams("arbitrary"),
        name="diff_attn_sample",
    )(table, proj, proj, proj, *([cache_k] * n_pages), *([cache_v] * n_pages), bias, lam, subln.reshape(1, -1))


S5_BLOCKS = 4
S5_BLOCK_STATES = (S5_GROUPS // S5_BLOCKS) * S5_STATE
S5_SEQ_ROWS = 8


def _gelu_tanh(x):
    return x * (0.5 * (1.0 + jnp.tanh(math.sqrt(2.0 / math.pi) * (x + 0.044715 * (x * x * x)))))


def _s5_kernel(u_ref, h0_ref, a_ref, bm_ref, cm_ref, d_ref, wglu_ref, bglu_ref, o_ref, hout_ref,
               utm_ref, hb_ref, otm_ref, hst_ref, *, n_seq, tb):
    t = pl.program_id(1)
    rows = S5_SEQ_ROWS * tb
    ns = S5_BLOCK_STATES

    @pl.when(t == 0)
    def _():
        hst_ref[...] = h0_ref[...]

    if n_seq < S5_SEQ_ROWS:
        utm_ref[...] = jnp.zeros(utm_ref.shape, F32)
    for k in range(S5_BLOCKS):
        for j in range(n_seq):
            utm_ref[k, pl.ds(j, tb, stride=S5_SEQ_ROWS), :] = u_ref[j, :, k * LANES:(k + 1) * LANES]
        hb_ref[k] = _dot(utm_ref[k].astype(BF16), bm_ref[k].astype(BF16))

    def step(i, hs):
        sel = pl.ds(pl.multiple_of(i * S5_SEQ_ROWS, S5_SEQ_ROWS), S5_SEQ_ROWS)
        new = []
        for k in range(S5_BLOCKS):
            h_re, h_im = hs[k]
            a_re = a_ref[k, 0:1, 0:ns]
            a_im = a_ref[k, 1:2, 0:ns]
            n_re = a_re * h_re - a_im * h_im + hb_ref[k, sel, 0:ns]
            n_im = a_re * h_im + a_im * h_re + hb_ref[k, sel, ns:2 * ns]
            hb_ref[k, sel, 0:ns] = n_re
            hb_ref[k, sel, ns:2 * ns] = n_im
            new.append((n_re, n_im))
        return tuple(new)

    hs = tuple((hst_ref[k, :, 0:ns], hst_ref[k, :, ns:2 * ns]) for k in range(S5_BLOCKS))
    hs = lax.fori_loop(0, tb, step, hs)
    for k in range(S5_BLOCKS):
        hst_ref[k, :, 0:ns] = hs[k][0]
        hst_ref[k, :, ns:2 * ns] = hs[k][1]

    ys = [_dot(hb_ref[k].astype(BF16), cm_ref[k].astype(BF16)) + d_ref[k] * utm_ref[k] for k in range(S5_BLOCKS)]
    z = _gelu_tanh(jnp.concatenate(ys, axis=1))
    gate = jax.nn.sigmoid(_dot(z.astype(BF16), wglu_ref[...].astype(BF16)) + bglu_ref[...])
    od = z * gate
    for k in range(S5_BLOCKS):
        otm_ref[k] = od[:, k * LANES:(k + 1) * LANES]
    for j in range(n_seq):
        for k in range(S5_BLOCKS):
            o_ref[j, :, k * LANES:(k + 1) * LANES] = otm_ref[k, pl.ds(j, tb, stride=S5_SEQ_ROWS), :]

    @pl.when(t == pl.num_programs(1) - 1)
    def _():
        hout_ref[...] = hst_ref[...]


def _s5_params(a_re, a_im, log_dt, b_re, b_im, c_re, c_im, d_skip):
    dt = jnp.exp(log_dt)[:, None]
    mag = jnp.exp(a_re * dt)
    ab_re, ab_im = mag * jnp.cos(a_im * dt), mag * jnp.sin(a_im * dt)
    den = a_re * a_re + a_im * a_im
    xr, xi = ab_re - 1.0, ab_im
    z_re = (xr * a_re + xi * a_im) / den
    z_im = (xi * a_re - xr * a_im) / den
    bb_re = z_re[..., None] * b_re - z_im[..., None] * b_im
    bb_im = z_re[..., None] * b_im + z_im[..., None] * b_re
    gb = S5_GROUPS // S5_BLOCKS
    eye = jnp.eye(gb, dtype=F32)

    def in_mat(bb):
        bb = bb.reshape(S5_BLOCKS, gb, S5_STATE, S5_GROUP_CH)
        return jnp.einsum('kgpc,gh->kgchp', bb, eye).reshape(S5_BLOCKS, gb * S5_GROUP_CH, gb * S5_STATE)

    def out_mat(cc):
        cc = cc.reshape(S5_BLOCKS, gb, S5_GROUP_CH, S5_STATE)
        return jnp.einsum('kgcp,gh->khpgc', cc, eye).reshape(S5_BLOCKS, gb * S5_STATE, gb * S5_GROUP_CH)

    bm = jnp.concatenate([in_mat(bb_re), in_mat(bb_im)], axis=2)
    cm = jnp.concatenate([out_mat(c_re), out_mat(-c_im)], axis=1)

    def lanes(x):
        x = x.reshape(S5_BLOCKS, gb * S5_STATE)
        return jnp.concatenate([x, x], axis=1)

    a = jnp.stack([lanes(ab_re), lanes(ab_im)], axis=1)
    dvec = d_skip.reshape(S5_BLOCKS, 1, gb * S5_GROUP_CH)
    return a, bm, cm, dvec


def _s5_state_to_blocks(h_re, h_im, n_rows):
    n = h_re.shape[0]
    gb = S5_GROUPS // S5_BLOCKS
    r = h_re.reshape(n, S5_BLOCKS, gb * S5_STATE)
    i = h_im.reshape(n, S5_BLOCKS, gb * S5_STATE)
    h = jnp.transpose(jnp.concatenate([r, i], axis=2), (1, 0, 2))
    if n_rows > n:
        h = jnp.concatenate([h, jnp.zeros((S5_BLOCKS, n_rows - n, h.shape[2]), F32)], axis=1)
    return h


def _s5_blocks_to_state(h, n):
    gb = S5_GROUPS // S5_BLOCKS
    h = jnp.transpose(h[:, :n], (1, 0, 2))
    re = h[:, :, :gb * S5_STATE].reshape(n, S5_GROUPS, S5_STATE)
    im = h[:, :, gb * S5_STATE:].reshape(n, S5_GROUPS, S5_STATE)
    return re, im


def s5(proj3, h0_blocks, params, wglu, bglu, *, seq_per_step, tb):
    n_seq, t_len, _ = proj3.shape
    a, bm, cm, dvec = params
    width = S5_GROUPS * S5_GROUP_CH
    ucol = proj3.shape[2] // width - 1
    n_groups = n_seq // seq_per_step
    rows = S5_SEQ_ROWS * tb
    kern = functools.partial(_s5_kernel, n_seq=seq_per_step, tb=tb)
    full = lambda shape: pl.BlockSpec(shape, lambda s, t: (0,) * len(shape))
    state = pl.BlockSpec((S5_BLOCKS, S5_SEQ_ROWS, 2 * S5_BLOCK_STATES), lambda s, t: (0, s, 0))
    return pl.pallas_call(
        kern,
        grid=(n_groups, t_len // tb),
        in_specs=[pl.BlockSpec((seq_per_step, tb, width), lambda s, t: (s, t, ucol)),
                  state, full(a.shape), full(bm.shape), full(cm.shape), full(dvec.shape),
                  full(wglu.shape), full((1, width))],
        out_specs=[pl.BlockSpec((seq_per_step, tb, width), lambda s, t: (s, t, 0)), state],
        out_shape=[jax.ShapeDtypeStruct((n_seq, t_len, width), F32),
                   jax.ShapeDtypeStruct(h0_blocks.shape, F32)],
        scratch_shapes=[pltpu.VMEM((S5_BLOCKS, rows, LANES), F32),
                        pltpu.VMEM((S5_BLOCKS, rows, 2 * S5_BLOCK_STATES), F32),
                        pltpu.VMEM((S5_BLOCKS, rows, LANES), F32),
                        pltpu.VMEM((S5_BLOCKS, S5_SEQ_ROWS, 2 * S5_BLOCK_STATES), F32)],
        compiler_params=_cparams("parallel", "arbitrary"),
        name="s5",
    )(proj3, h0_blocks, a, bm, cm, dvec, wglu, bglu.reshape(1, width))


def _trunk(x, p, past, *, n_seq, t_len):
    n = n_seq * t_len
    tm = min(n, 1024)
    prompt = past is None
    a_width = A_HEADS * HEAD_DIM
    c_width = C_HEADS * 2 * C_HD
    new = {}

    proj = rms_matmul(x, p['norm_mix'][0], p['w_in_even'][0], tm=tm, tn=512)
    ka = proj[:, a_width:2 * a_width].reshape(n_seq, t_len, A_HEADS, HEAD_DIM)
    va = proj[:, 2 * a_width:3 * a_width].reshape(n_seq, t_len, A_HEADS, HEAD_DIM)
    lb = jnp.cumsum(jax.nn.softmax(p['hgrn_lb'].astype(F32), axis=0), axis=0)[0]
    if prompt:
        keep = min(A_MAX_WINDOW, t_len)
        new['a_k'], new['a_v'] = ka[:, t_len - keep:], va[:, t_len - keep:]
        o_a = attn_a_prompt(proj, n_seq, t_len)
        s0 = jnp.zeros((n_seq, B_HEADS, B_DK, B_DK), F32)
        o_b, s_fin = hgrn(proj, lb, p['hgrn_gnorm'][0], s0, n_seq=n_seq, t_len=t_len,
                          seq_per_step=1, t_blk=512, chunk=64, sub=16)
    else:
        new['a_k'], new['a_v'] = ka, va
        o_a = attn_a_sample(proj, jnp.transpose(past['a_k'][0], (0, 2, 3, 1)),
                            jnp.transpose(past['a_v'][0], (0, 2, 3, 1)), n_seq, t_len)
        o_b, s_fin = hgrn(proj, lb, p['hgrn_gnorm'][0], past['hgrn'][0], n_seq=n_seq, t_len=t_len,
                          seq_per_step=16, t_blk=t_len, chunk=16, sub=16)
    new['hgrn'] = s_fin
    x = mix_out(x, o_a, o_b, p['w_out_even'][0], tm=tm)
    x = ffn(x, p['norm_ffn'][0], p['ffn_w_gate'][0], p['ffn_w_up'][0], p['ffn_w_down'][0], tm=tm, tf=256)

    proj = rms_matmul(x, p['norm_mix'][1], p['w_in_odd'][0], tm=tm, tn=512)
    new['c_k'] = proj[:, c_width:2 * c_width].reshape(n_seq, t_len, C_HEADS, 2 * C_HD)
    new['c_v'] = proj[:, 2 * c_width:3 * c_width].reshape(n_seq, t_len, C_HEADS, 2 * C_HD)
    lam_init = 0.8 - 0.6 * math.exp(-0.3 * 1)
    lam = (jnp.exp(jnp.sum(p['diff_lq1'][0] * p['diff_lk1'][0]))
           - jnp.exp(jnp.sum(p['diff_lq2'][0] * p['diff_lk2'][0])) + lam_init)
    lam = jnp.full((1, LANES), lam, F32)
    s5_params = _s5_params(p['s5_a_re'][0], p['s5_a_im'][0], p['s5_log_dt'][0], p['s5_b_re'][0], p['s5_b_im'][0],
                           p['s5_c_re'][0], p['s5_c_im'][0], p['s5_d'][0])
    proj3 = proj.reshape(n_seq, t_len, proj.shape[1])
    if prompt:
        slopes = jnp.asarray(np.broadcast_to(_alibi_slopes(C_HEADS)[:, None, None], (C_HEADS, 1, LANES)), F32)
        o_c = diff_attn_prompt(proj, slopes, lam, p['diff_subln'][0], 1.0 - lam_init, n_seq, t_len, tq=512, tk=1024)
        h0 = jnp.zeros((S5_BLOCKS, S5_SEQ_ROWS, 2 * S5_BLOCK_STATES), F32)
        o_d, h_fin = s5(proj3, h0, s5_params, p['s5_w_glu'][0], p['s5_b_glu'][0], seq_per_step=n_seq, tb=128)
    else:
        n_phys = past['c_k'].shape[1]
        page_rows = (n_phys, PAGE_SIZE * C_HEADS, 2 * C_HD)
        o_c = diff_attn_sample(proj, past['c_k'][0].reshape(page_rows), past['c_v'][0].reshape(page_rows),
                               past['page_table'], lam, p['diff_subln'][0], 1.0 - lam_init, n_seq, t_len)
        h0 = _s5_state_to_blocks(past['s5_re'][0], past['s5_im'][0], n_seq)
        o_d, h_fin = s5(proj3, h0, s5_params, p['s5_w_glu'][0], p['s5_b_glu'][0],
                        seq_per_step=S5_SEQ_ROWS, tb=t_len)
    new['s5_re'], new['s5_im'] = _s5_blocks_to_state(h_fin, n_seq)
    x = mix_out(x, o_c, o_d.reshape(n, -1), p['w_out_odd'][0], tm=tm)
    return x, {k: v[None] for k, v in new.items()}


def kernel(x_prompt, x_sample, cache_a_k, cache_a_v, state_hgrn, cache_c_k, cache_c_v, state_s5_re, state_s5_im,
           page_table, norm_mix, norm_ffn, norm_final, w_in_even, w_out_even, hgrn_lb, hgrn_gnorm,
           ffn_w_gate, ffn_w_up, ffn_w_down, w_in_odd, w_out_odd, diff_lq1, diff_lk1, diff_lq2, diff_lk2,
           diff_subln, s5_a_re, s5_a_im, s5_log_dt, s5_b_re, s5_b_im, s5_c_re, s5_c_im, s5_d, s5_w_glu,
           s5_b_glu, moe_router_w, moe_router_b, moe_w_gate, moe_w_up, moe_w_down):
    p = {
        'norm_mix': norm_mix, 'norm_ffn': norm_ffn, 'norm_final': norm_final,
        'w_in_even': w_in_even, 'w_out_even': w_out_even, 'hgrn_lb': hgrn_lb, 'hgrn_gnorm': hgrn_gnorm,
        'ffn_w_gate': ffn_w_gate, 'ffn_w_up': ffn_w_up, 'ffn_w_down': ffn_w_down,
        'w_in_odd': w_in_odd, 'w_out_odd': w_out_odd, 'diff_lq1': diff_lq1, 'diff_lk1': diff_lk1,
        'diff_lq2': diff_lq2, 'diff_lk2': diff_lk2, 'diff_subln': diff_subln,
        's5_a_re': s5_a_re, 's5_a_im': s5_a_im, 's5_log_dt': s5_log_dt, 's5_b_re': s5_b_re, 's5_b_im': s5_b_im,
        's5_c_re': s5_c_re, 's5_c_im': s5_c_im, 's5_d': s5_d, 's5_w_glu': s5_w_glu, 's5_b_glu': s5_b_glu,
        'moe_router_w': moe_router_w, 'moe_router_b': moe_router_b,
        'moe_w_gate': moe_w_gate, 'moe_w_up': moe_w_up, 'moe_w_down': moe_w_down,
    }
    past = {'a_k': cache_a_k, 'a_v': cache_a_v, 'hgrn': state_hgrn, 'c_k': cache_c_k, 'c_v': cache_c_v,
            's5_re': state_s5_re, 's5_im': state_s5_im, 'page_table': page_table}
    bp, tp, _ = x_prompt.shape
    bs, ts, _ = x_sample.shape
    h_prompt, sp = _trunk(x_prompt.reshape(bp * tp, D_MODEL), p, None, n_seq=bp, t_len=tp)
    h_sample, ss = _trunk(x_sample.reshape(bs * ts, D_MODEL), p, past, n_seq=bs, t_len=ts)
    y_prompt, y_sample = moe_block([h_prompt, h_sample], p['norm_ffn'][1], p['moe_router_w'][0], p['moe_router_b'][0],
                                   p['moe_w_gate'][0], p['moe_w_up'][0], p['moe_w_down'][0], p['norm_final'],
                                   tm_route=512, tm_expert=1024, tf=256, tm_combine=256)
    y_prompt = y_prompt.reshape(bp, tp, D_MODEL)
    y_sample = y_sample.reshape(bs, ts, D_MODEL)
    names = ('a_k', 'a_v', 'hgrn', 'c_k', 'c_v', 's5_re', 's5_im')
    return (y_prompt, y_sample) + tuple(sp[k] for k in names) + tuple(ss[k] for k in names)
```
